```python
import jax, jax.numpy as jnp
from jax import lax
import numpy as np

D_MODEL = 4096
BATCH = 4
SEQ = 2048
DEPTH = 4

N_MIXERS = 2
MEM_TOKENS = 256
MIX_WIDTH = D_MODEL
MEM_HEADS = 4
MEM_HEAD_DIM = MIX_WIDTH // 4 // MEM_HEADS
TOKEN_MIX_WIDTH = MIX_WIDTH - MEM_HEADS * MEM_HEAD_DIM
MLSTM_HEAD_V = 512
MLSTM_HEADS = TOKEN_MIX_WIDTH // MLSTM_HEAD_V
MLSTM_HEAD_QK = MLSTM_HEAD_V // 2
MLSTM_CHUNK = 64
GATE_SOFT_CAP = 15.0
SB_HEAD_DIM = 128
SB_HEADS = TOKEN_MIX_WIDTH // SB_HEAD_DIM
SB_BLOCK = 128
D_FF = 3 * D_MODEL // 2
RMS_EPS = 1e-6
N_A_LAYERS = (DEPTH + 1) // 2
N_B_LAYERS = DEPTH // 2

A_SPLITS = (MLSTM_HEADS * MLSTM_HEAD_QK, MLSTM_HEADS * MLSTM_HEAD_QK, TOKEN_MIX_WIDTH,
            TOKEN_MIX_WIDTH, MLSTM_HEADS, MLSTM_HEADS, MEM_HEADS * MEM_HEAD_DIM)
A_IN_COLS = sum(A_SPLITS)
B_SPLITS = (TOKEN_MIX_WIDTH, TOKEN_MIX_WIDTH, TOKEN_MIX_WIDTH, MEM_HEADS * MEM_HEAD_DIM)
B_IN_COLS = sum(B_SPLITS)

kernel_name = "hybrid_mlstm_stickbreaking_macaron_memory"


def rmsnorm(x, g):
    xf = x.astype(jnp.float32)
    y = xf * lax.rsqrt(jnp.mean(xf * xf, axis=-1, keepdims=True) + RMS_EPS)
    return (y * g.astype(jnp.float32)).astype(x.dtype)


def _split(t, sizes):
    idx = np.cumsum(sizes)[:-1].tolist()
    return jnp.split(t, idx, axis=-1)


def _heads(t, n_heads):
    b, s, _ = t.shape
    return t.reshape(b, s, n_heads, -1).transpose(0, 2, 1, 3)


def _merge_heads(t):
    b, h, s, d = t.shape
    return t.transpose(0, 2, 1, 3).reshape(b, s, h * d)


def swiglu(x, w_in, w_out):
    g, u = jnp.split(x @ w_in, 2, axis=-1)
    return (jax.nn.silu(g) * u) @ w_out


def softcap(t):
    return GATE_SOFT_CAP * jnp.tanh(t / GATE_SOFT_CAP)


def memory_attention(mq, mem_k, mem_v):
    b, s, _ = mq.shape
    q = mq.reshape(b, s, MEM_HEADS, MEM_HEAD_DIM)
    scores = jnp.einsum('bshd,bmhd->bhsm', q, mem_k).astype(jnp.float32) * (MEM_HEAD_DIM ** -0.5)
    p = jax.nn.softmax(scores, axis=-1).astype(mem_v.dtype)
    o = jnp.einsum('bhsm,bmhd->bshd', p, mem_v)
    return o.reshape(b, s, MEM_HEADS * MEM_HEAD_DIM)


def _mlstm_chunk_step(carry, inp):
    c_prev, n_prev, m_prev = carry
    q, k, v, ig, lf = inp
    L = q.shape[2]
    b = jnp.cumsum(lf, axis=-1)
    causal = jnp.tril(jnp.ones((L, L), dtype=bool))
    d_log = jnp.where(causal, b[..., :, None] - b[..., None, :] + ig[..., None, :], -jnp.inf)
    inter_log = b + m_prev[..., None]
    m_t = jnp.maximum(inter_log, d_log.max(axis=-1))
    d_w = jnp.exp(d_log - m_t[..., None])
    inter_w = jnp.exp(inter_log - m_t)
    s = jnp.einsum('bhjd,bhsd->bhjs', q, k) * d_w
    num = jnp.einsum('bhjs,bhsv->bhjv', s, v) + inter_w[..., None] * jnp.einsum('bhjd,bhvd->bhjv', q, c_prev)
    den = s.sum(axis=-1) + inter_w * jnp.einsum('bhjd,bhd->bhj', q, n_prev)
    h = num / jnp.maximum(jnp.abs(den), jnp.exp(-m_t))[..., None]
    b_last = b[..., -1]
    w_log = b_last[..., None] - b + ig
    m_new = jnp.maximum(b_last + m_prev, w_log.max(axis=-1))
    decay = jnp.exp(b_last + m_prev - m_new)
    w = jnp.exp(w_log - m_new[..., None])
    c_new = decay[..., None, None] * c_prev + jnp.einsum('bhs,bhsv,bhsd->bhvd', w, v, k)
    n_new = decay[..., None] * n_prev + jnp.einsum('bhs,bhsd->bhd', w, k)
    return (c_new, n_new, m_new), h


def mlstm_chunkwise(q, k, v, ig, lf):
    b, h, s, dk = q.shape
    dv = v.shape[-1]
    nc = s // MLSTM_CHUNK

    def chunks(t):
        return jnp.moveaxis(t.reshape(b, h, nc, MLSTM_CHUNK, *t.shape[3:]), 2, 0)

    init = (jnp.zeros((b, h, dv, dk), jnp.float32),
            jnp.zeros((b, h, dk), jnp.float32),
            jnp.zeros((b, h), jnp.float32))
    _, hs = lax.scan(_mlstm_chunk_step, init, (chunks(q), chunks(k), chunks(v), chunks(ig), chunks(lf)))
    return jnp.moveaxis(hs, 0, 2).reshape(b, h, s, dv)


def mlstm_mixer(xn, w_in, b_igate, b_fgate, head_gain, mem_k, mem_v):
    q, k, v, o, ig, fg, mq = _split(xn @ w_in, A_SPLITS)
    f32 = jnp.float32
    q = _heads(q, MLSTM_HEADS).astype(f32) * (MLSTM_HEAD_QK ** -0.5)
    k = _heads(k, MLSTM_HEADS).astype(f32)
    v = _heads(v, MLSTM_HEADS).astype(f32)
    ig = softcap((ig + b_igate).astype(f32)).transpose(0, 2, 1)
    lf = jax.nn.log_sigmoid(softcap((fg + b_fgate).astype(f32))).transpose(0, 2, 1)
    h = mlstm_chunkwise(q, k, v, ig, lf)
    h = h.transpose(0, 2, 1, 3)
    h = rmsnorm(h, head_gain.reshape(MLSTM_HEADS, MLSTM_HEAD_V))
    b_, s_ = h.shape[:2]
    h = (jax.nn.sigmoid(o.astype(f32)) * h.reshape(b_, s_, TOKEN_MIX_WIDTH)).astype(xn.dtype)
    return jnp.concatenate([h, memory_attention(mq, mem_k, mem_v)], axis=-1)


def stick_breaking_attention(q, k, v):
    _, _, s_len, d = q.shape
    scale = d ** -0.5
    outs = []
    for blk in range(s_len // SB_BLOCK):
        q0 = blk * SB_BLOCK
        q1 = q0 + SB_BLOCK
        z = jnp.einsum('bhtd,bhsd->bhts', q[:, :, q0:q1], k[:, :, :q1]).astype(jnp.float32) * scale
        t_pos = q0 + jnp.arange(SB_BLOCK)
        s_pos = jnp.arange(q1)
        causal = s_pos[None, :] < t_pos[:, None]
        log_keep = jnp.where(causal, jax.nn.log_sigmoid(-z), 0.0)
        later = lax.cumsum(log_keep, axis=3, reverse=True) - log_keep
        log_a = jax.nn.log_sigmoid(z) + later
        a = jnp.where(causal, jnp.exp(log_a), 0.0).astype(v.dtype)
        outs.append(jnp.einsum('bhts,bhsd->bhtd', a, v[:, :, :q1]))
    return jnp.concatenate(outs, axis=2)


def stick_breaking_mixer(xn, w_in, mem_k, mem_v):
    q, k, v, mq = _split(xn @ w_in, B_SPLITS)
    o = stick_breaking_attention(_heads(q, SB_HEADS), _heads(k, SB_HEADS), _heads(v, SB_HEADS))
    return jnp.concatenate([_merge_heads(o), memory_attention(mq, mem_k, mem_v)], axis=-1)


def setup_inputs(seed: int = 0) -> dict:
    key = jax.random.key(seed)
    ks = jax.random.split(key, 20)

    def nrm(k, shape, scale):
        return jax.random.normal(k, shape, jnp.float32) * scale

    return {
        "x": nrm(ks[0], (BATCH, SEQ, D_MODEL), 1.0),
        "mem": nrm(ks[1], (BATCH, MEM_TOKENS, D_MODEL), 1.0),
        "norm_ffn1": 1.0 + nrm(ks[2], (DEPTH, D_MODEL), 0.02),
        "ffn1_w_in": nrm(ks[3], (DEPTH, D_MODEL, 2 * D_FF), D_MODEL ** -0.5),
        "ffn1_w_out": nrm(ks[4], (DEPTH, D_FF, D_MODEL), D_FF ** -0.5),
        "norm_mix": 1.0 + nrm(ks[5], (DEPTH, D_MODEL), 0.02),
        "a_w_in": nrm(ks[6], (N_A_LAYERS, D_MODEL, A_IN_COLS), D_MODEL ** -0.5),
        "a_b_igate": nrm(ks[7], (N_A_LAYERS, MLSTM_HEADS), 0.1),
        "a_b_fgate": 3.0 + nrm(ks[8], (N_A_LAYERS, MLSTM_HEADS), 0.5),
        "a_head_gain": 1.0 + nrm(ks[9], (N_A_LAYERS, TOKEN_MIX_WIDTH), 0.02),
        "b_w_in": nrm(ks[10], (N_B_LAYERS, D_MODEL, B_IN_COLS), D_MODEL ** -0.5),
        "mem_norm": 1.0 + nrm(ks[11], (D_MODEL,), 0.02),
        "w_mem_kv": nrm(ks[12], (D_MODEL, 2 * MEM_HEADS * MEM_HEAD_DIM), D_MODEL ** -0.5),
        "w_out": nrm(ks[13], (DEPTH, MIX_WIDTH, D_MODEL), MIX_WIDTH ** -0.5),
        "norm_ffn2": 1.0 + nrm(ks[14], (DEPTH, D_MODEL), 0.02),
        "ffn2_w_in": nrm(ks[15], (DEPTH, D_MODEL, 2 * D_FF), D_MODEL ** -0.5),
        "ffn2_w_out": nrm(ks[16], (DEPTH, D_FF, D_MODEL), D_FF ** -0.5),
        "norm_final": 1.0 + nrm(ks[17], (D_MODEL,), 0.02),
    }


def reference(x, mem, norm_ffn1, ffn1_w_in, ffn1_w_out, norm_mix, a_w_in, a_b_igate, a_b_fgate,
              a_head_gain, b_w_in, mem_norm, w_mem_kv, w_out, norm_ffn2, ffn2_w_in, ffn2_w_out,
              norm_final):
    b, m_len, _ = mem.shape
    mem_k, mem_v = jnp.split(rmsnorm(mem, mem_norm) @ w_mem_kv, 2, axis=-1)
    mem_k = mem_k.reshape(b, m_len, MEM_HEADS, MEM_HEAD_DIM)
    mem_v = mem_v.reshape(b, m_len, MEM_HEADS, MEM_HEAD_DIM)

    h = x
    for i in range(DEPTH):
        h = h + 0.5 * swiglu(rmsnorm(h, norm_ffn1[i]), ffn1_w_in[i], ffn1_w_out[i])
        xn = rmsnorm(h, norm_mix[i])
        j = i // N_MIXERS
        if i % N_MIXERS == 0:
            mixed = mlstm_mixer(xn, a_w_in[j], a_b_igate[j], a_b_fgate[j], a_head_gain[j], mem_k, mem_v)
        else:
            mixed = stick_breaking_mixer(xn, b_w_in[j], mem_k, mem_v)
        h = h + mixed @ w_out[i]
        h = h + 0.5 * swiglu(rmsnorm(h, norm_ffn2[i]), ffn2_w_in[i], ffn2_w_out[i])
    return rmsnorm(h, norm_final)
```

```python
import functools

import jax
import jax.numpy as jnp
from jax import lax
from jax.experimental import pallas as pl
from jax.experimental.pallas import tpu as pltpu

F32 = jnp.float32
BF16 = jnp.bfloat16

RMS_EPS = 1e-6
GATE_SOFT_CAP = 15.0
MEM_HEADS = 4
MLSTM_HEAD_V = 512
MLSTM_HEAD_QK = 256
SB_HEAD_DIM = 128

V7X_VMEM_LIMIT_BYTES = 56 * 1024 * 1024
LANES = 128
MLSTM_TILE = 128
SB_Q_TILE = 128
SB_K_TILE = 256
GATE_COLS = 128
GATE_ROWS = 16


def _params(*sem):
    return pltpu.CompilerParams(dimension_semantics=sem, vmem_limit_bytes=V7X_VMEM_LIMIT_BYTES)


def _dot(a, b):
    return jnp.dot(a, b, preferred_element_type=F32)


def _dot_nt(a, b):
    return lax.dot_general(a, b, (((1,), (1,)), ((), ())), preferred_element_type=F32)


def _dot_tn(a, b):
    return lax.dot_general(a, b, (((0,), (0,)), ((), ())), preferred_element_type=F32)


def _sigmoid(x):
    return 1.0 / (1.0 + jnp.exp(-x))


def _log_sigmoid(x):
    return jnp.minimum(x, 0.0) - jnp.log1p(jnp.exp(-jnp.abs(x)))


def _split3(x):
    hi = x.astype(BF16)
    r1 = x - hi.astype(F32)
    mid = r1.astype(BF16)
    lo = (r1 - mid.astype(F32)).astype(BF16)
    return hi, mid, lo


def _rmsnorm_kernel(x_ref, g_ref, o_ref):
    x = x_ref[...]
    ms = jnp.mean(x * x, axis=-1, keepdims=True)
    o_ref[...] = ((x * lax.rsqrt(ms + RMS_EPS)) * g_ref[...]).astype(o_ref.dtype)


def _rmsnorm(x, g, out_dtype, rows=256):
    m, d = x.shape
    rows = min(rows, m)
    return pl.pallas_call(
        _rmsnorm_kernel,
        grid=(m // rows,),
        in_specs=[pl.BlockSpec((rows, d), lambda i: (i, 0)),
                  pl.BlockSpec((1, d), lambda i: (0, 0))],
        out_specs=pl.BlockSpec((rows, d), lambda i: (i, 0)),
        out_shape=jax.ShapeDtypeStruct((m, d), out_dtype),
        compiler_params=_params("parallel"),
        name="rmsnorm",
    )(x, g.reshape(1, d))


def _mm_kernel(a_ref, w_ref, o_ref):
    o_ref[...] = _dot(a_ref[...], w_ref[...]).astype(o_ref.dtype)


def _matmul(a, w, out_dtype, tm=1024, tn=1024):
    m, k = a.shape
    n = w.shape[1]
    tm, tn = min(tm, m), min(tn, n)
    return pl.pallas_call(
        _mm_kernel,
        grid=(m // tm, n // tn),
        in_specs=[pl.BlockSpec((tm, k), lambda i, j: (i, 0)),
                  pl.BlockSpec((k, tn), lambda i, j: (0, j))],
        out_specs=pl.BlockSpec((tm, tn), lambda i, j: (i, j)),
        out_shape=jax.ShapeDtypeStruct((m, n), out_dtype),
        compiler_params=_params("parallel", "arbitrary"),
        name="matmul",
    )(a, w)


def _swiglu_in_kernel(a_ref, wg_ref, wu_ref, o_ref):
    a = a_ref[...]
    g = _dot(a, wg_ref[...])
    u = _dot(a, wu_ref[...])
    o_ref[...] = ((g * _sigmoid(g)) * u).astype(o_ref.dtype)


def _swiglu_in(a, w_in, tm=1024, tn=512):
    m, k = a.shape
    f = w_in.shape[1] // 2
    tm, tn = min(tm, m), min(tn, f)
    nblk = f // tn
    return pl.pallas_call(
        _swiglu_in_kernel,
        grid=(m // tm, nblk),
        in_specs=[pl.BlockSpec((tm, k), lambda i, j: (i, 0)),
                  pl.BlockSpec((k, tn), lambda i, j: (0, j)),
                  pl.BlockSpec((k, tn), lambda i, j: (0, j + nblk))],
        out_specs=pl.BlockSpec((tm, tn), lambda i, j: (i, j)),
        out_shape=jax.ShapeDtypeStruct((m, f), BF16),
        compiler_params=_params("parallel", "arbitrary"),
        name="swiglu_in",
    )(a, w_in, w_in)


def _mm_residual_kernel(a_ref, w_ref, r_ref, o_ref, *, scale):
    o_ref[...] = r_ref[...] + scale * _dot(a_ref[...], w_ref[...])


def _matmul_residual(a, w, res, scale, tm=1024, tn=512):
    m, k = a.shape
    n = w.shape[1]
    tm, tn = min(tm, m), min(tn, n)
    return pl.pallas_call(
        functools.partial(_mm_residual_kernel, scale=scale),
        grid=(m // tm, n // tn),
        in_specs=[pl.BlockSpec((tm, k), lambda i, j: (i, 0)),
                  pl.BlockSpec((k, tn), lambda i, j: (0, j)),
                  pl.BlockSpec((tm, tn), lambda i, j: (i, j))],
        out_specs=pl.BlockSpec((tm, tn), lambda i, j: (i, j)),
        out_shape=jax.ShapeDtypeStruct((m, n), F32),
        compiler_params=_params("parallel", "arbitrary"),
        name="matmul_residual",
    )(a, w, res)


def _mm2_residual_kernel(a1_ref, a2_ref, w1_ref, w2_ref, r_ref, o_ref):
    acc = _dot(a1_ref[...], w1_ref[...]) + _dot(a2_ref[...], w2_ref[...])
    o_ref[...] = r_ref[...] + acc


def _out_proj_residual(a1, a2, w, res, tm=1024, tn=512):
    m, k1 = a1.shape
    k2 = a2.shape[1]
    n = w.shape[1]
    assert k1 % k2 == 0 and w.shape[0] == k1 + k2
    tm, tn = min(tm, m), min(tn, n)
    return pl.pallas_call(
        _mm2_residual_kernel,
        grid=(m // tm, n // tn),
        in_specs=[pl.BlockSpec((tm, k1), lambda i, j: (i, 0)),
                  pl.BlockSpec((tm, k2), lambda i, j: (i, 0)),
                  pl.BlockSpec((k1, tn), lambda i, j: (0, j)),
                  pl.BlockSpec((k2, tn), lambda i, j: (k1 // k2, j)),
                  pl.BlockSpec((tm, tn), lambda i, j: (i, j))],
        out_specs=pl.BlockSpec((tm, tn), lambda i, j: (i, j)),
        out_shape=jax.ShapeDtypeStruct((m, n), F32),
        compiler_params=_params("parallel", "arbitrary"),
        name="out_proj_residual",
    )(a1, a2, w, w, res)


def _gates_kernel(xn_ref, wg_ref, wgt_ref, brow_ref, bcol_ref, col_ref, row_ref, *, n_heads, tile):
    x = xn_ref[...]
    rows = x.shape[0]
    g_col = _dot(x, wg_ref[...]) + brow_ref[...]
    g_row = _dot_nt(wgt_ref[...], x) + bcol_ref[...]

    def transform(t, is_forget):
        capped = GATE_SOFT_CAP * jnp.tanh(t / GATE_SOFT_CAP)
        return jnp.where(is_forget, _log_sigmoid(capped), capped)

    lane = lax.broadcasted_iota(jnp.int32, g_col.shape, 1)
    sub = lax.broadcasted_iota(jnp.int32, g_row.shape, 0)
    t_col = transform(g_col, lane >= n_heads)
    t_row = transform(g_row, sub >= n_heads)

    r = lax.broadcasted_iota(jnp.int32, (tile, tile), 0)
    c = lax.broadcasted_iota(jnp.int32, (tile, tile), 1)
    lower = (c <= r).astype(BF16)
    upper = (r <= c).astype(BF16)
    forget_lane = lax.broadcasted_iota(jnp.int32, (tile, g_col.shape[1]), 1) >= n_heads
    forget_sub = lax.broadcasted_iota(jnp.int32, (g_row.shape[0], tile), 0) >= n_heads

    for t0 in range(0, rows, tile):
        blk = t_col[t0:t0 + tile, :]
        hi, mid, lo = _split3(blk)
        cs = _dot(lower, hi) + _dot(lower, mid) + _dot(lower, lo)
        col_ref[t0:t0 + tile, :] = jnp.where(forget_lane, cs, blk)

        blk = t_row[:, t0:t0 + tile]
        hi, mid, lo = _split3(blk)
        cs = _dot(hi, upper) + _dot(mid, upper) + _dot(lo, upper)
        row_ref[:, t0:t0 + tile] = jnp.where(forget_sub, cs, blk)


def _mlstm_gates(xn, w_gates, b_ig, b_fg, tile, rows=512):
    m, d = xn.shape
    n_heads = b_ig.shape[0]
    rows = min(rows, m)
    wg = jnp.zeros((d, GATE_COLS), BF16).at[:, :2 * n_heads].set(w_gates.astype(BF16))
    wgt = jnp.zeros((GATE_ROWS, d), BF16).at[:2 * n_heads, :].set(w_gates.T.astype(BF16))
    bias = jnp.concatenate([b_ig, b_fg]).astype(F32)
    brow = jnp.zeros((1, GATE_COLS), F32).at[0, :2 * n_heads].set(bias)
    bcol = jnp.zeros((GATE_ROWS, 1), F32).at[:2 * n_heads, 0].set(bias)
    return pl.pallas_call(
        functools.partial(_gates_kernel, n_heads=n_heads, tile=tile),
        grid=(m // rows,),
        in_specs=[pl.BlockSpec((rows, d), lambda i: (i, 0)),
                  pl.BlockSpec((d, GATE_COLS), lambda i: (0, 0)),
                  pl.BlockSpec((GATE_ROWS, d), lambda i: (0, 0)),
                  pl.BlockSpec((1, GATE_COLS), lambda i: (0, 0)),
                  pl.BlockSpec((GATE_ROWS, 1), lambda i: (0, 0))],
        out_specs=[pl.BlockSpec((rows, GATE_COLS), lambda i: (i, 0)),
                   pl.BlockSpec((GATE_ROWS, rows), lambda i: (0, i))],
        out_shape=[jax.ShapeDtypeStruct((m, GATE_COLS), F32),
                   jax.ShapeDtypeStruct((GATE_ROWS, m), F32)],
        compiler_params=_params("parallel"),
        name="mlstm_gates",
    )(xn, wg, wgt, brow, bcol)


def _mlstm_kernel(q_ref, k_ref, v_ref, o_ref, gcol_ref, grow_ref, gain_ref, out_ref,
                  c_ref, n_ref, m_ref, *, n_heads, dk, dv):
    @pl.when(pl.program_id(1) == 0)
    def _():
        c_ref[...] = jnp.zeros_like(c_ref)
        n_ref[...] = jnp.zeros_like(n_ref)
        m_ref[...] = jnp.zeros_like(m_ref)

    t = q_ref.shape[0]
    r = lax.broadcasted_iota(jnp.int32, (t, t), 0)
    c = lax.broadcasted_iota(jnp.int32, (t, t), 1)
    causal = c <= r
    gcol = gcol_ref[...]
    grow = grow_ref[...]

    for h in range(n_heads):
        q = q_ref[:, h * dk:(h + 1) * dk] * (dk ** -0.5)
        k = k_ref[:, h * dk:(h + 1) * dk]
        v = v_ref[:, h * dv:(h + 1) * dv]
        qb, kb, vb = q.astype(BF16), k.astype(BF16), v.astype(BF16)
        ig_col = gcol[:, h:h + 1]
        b_col = gcol[:, n_heads + h:n_heads + h + 1]
        ig_row = grow[h:h + 1, :]
        b_row = grow[n_heads + h:n_heads + h + 1, :]
        c_prev = c_ref[h]
        n_prev = n_ref[h]
        m_prev = m_ref[h]

        d_log = jnp.where(causal, (b_col - b_row) + ig_row, -jnp.inf)
        inter_log = b_col + m_prev
        m_t = jnp.maximum(inter_log, jnp.max(d_log, axis=1, keepdims=True))
        d_w = jnp.exp(d_log - m_t)
        inter_w = jnp.exp(inter_log - m_t)
        s = _dot_nt(qb, kb) * d_w
        num = _dot(s.astype(BF16), vb) + inter_w * _dot(qb, c_prev.astype(BF16))
        den = jnp.sum(s, axis=1, keepdims=True) + inter_w * jnp.sum(q * n_prev, axis=1, keepdims=True)
        hh = num / jnp.maximum(jnp.abs(den), jnp.exp(-m_t))

        b_last = b_col[t - 1:t, :]
        w_log = (b_last - b_col) + ig_col
        m_new = jnp.maximum(b_last + m_prev, jnp.max(w_log, axis=0, keepdims=True))
        decay = jnp.exp((b_last + m_prev) - m_new)
        w = jnp.exp(w_log - m_new)
        c_ref[h] = decay * c_prev + _dot_tn(kb, (w * v).astype(BF16))
        n_ref[h] = decay * n_prev + jnp.sum(w * k, axis=0, keepdims=True)
        m_ref[h] = m_new

        ms = jnp.mean(hh * hh, axis=-1, keepdims=True)
        y = (hh * lax.rsqrt(ms + RMS_EPS)) * gain_ref[:, h * dv:(h + 1) * dv]
        o = o_ref[:, h * dv:(h + 1) * dv]
        out_ref[:, h * dv:(h + 1) * dv] = (_sigmoid(o) * y).astype(out_ref.dtype)


def _mlstm(proj, gcol, grow, head_gain, batch, seq, n_heads):
    dk, dv, t = MLSTM_HEAD_QK, MLSTM_HEAD_V, MLSTM_TILE
    wqk, wv = n_heads * dk, n_heads * dv
    assert 2 * wqk == wv
    nt = seq // t
    row = lambda b, i: b * nt + i
    return pl.pallas_call(
        functools.partial(_mlstm_kernel, n_heads=n_heads, dk=dk, dv=dv),
        grid=(batch, nt),
        in_specs=[pl.BlockSpec((t, wqk), lambda b, i: (row(b, i), 0)),
                  pl.BlockSpec((t, wqk), lambda b, i: (row(b, i), 1)),
                  pl.BlockSpec((t, wv), lambda b, i: (row(b, i), 1)),
                  pl.BlockSpec((t, wv), lambda b, i: (row(b, i), 2)),
                  pl.BlockSpec((t, GATE_COLS), lambda b, i: (row(b, i), 0)),
                  pl.BlockSpec((GATE_ROWS, t), lambda b, i: (0, row(b, i))),
                  pl.BlockSpec((1, wv), lambda b, i: (0, 0))],
        out_specs=pl.BlockSpec((t, wv), lambda b, i: (row(b, i), 0)),
        out_shape=jax.ShapeDtypeStruct((batch * seq, wv), BF16),
        scratch_shapes=[pltpu.VMEM((n_heads, dk, dv), F32),
                        pltpu.VMEM((n_heads, 1, dk), F32),
                        pltpu.VMEM((n_heads, 1, 1), F32)],
        compiler_params=_params("parallel", "arbitrary"),
        name="mlstm",
    )(proj, proj, proj, proj, gcol, grow, head_gain.reshape(1, wv))


def _sb_kernel(q_ref, k_ref, v_ref, o_ref, *, scale, tk):
    tq, d = q_ref.shape
    qi = pl.program_id(2)
    qb = q_ref[...].astype(BF16)
    r = lax.broadcasted_iota(jnp.int32, (tk, tk), 0)
    c = lax.broadcasted_iota(jnp.int32, (tk, tk), 1)
    strictly_later = (r > c).astype(BF16)

    def tile(j, carry, acc, masked):
        k0 = pl.multiple_of(j * tk, tk)
        kb = k_ref[pl.ds(k0, tk), :].astype(BF16)
        vb = v_ref[pl.ds(k0, tk), :].astype(BF16)
        z = _dot_nt(qb, kb) * scale
        log_beta = _log_sigmoid(z)
        log_keep = log_beta - z
        if masked:
            t_pos = qi * tq + lax.broadcasted_iota(jnp.int32, (tq, tk), 0)
            s_pos = j * tk + lax.broadcasted_iota(jnp.int32, (tq, tk), 1)
            causal = s_pos < t_pos
            log_keep = jnp.where(causal, log_keep, 0.0)
        hi, mid, lo = _split3(log_keep)
        later = (_dot(hi, strictly_later) + _dot(mid, strictly_later) + _dot(lo, strictly_later)) + carry
        a = jnp.exp(log_beta + later)
        if masked:
            a = jnp.where(causal, a, 0.0)
        acc = acc + _dot(a.astype(BF16), vb)
        carry = carry + jnp.sum(log_keep, axis=1, keepdims=True)
        return carry, acc

    j_diag = (qi * tq + tq - 1) // tk
    carry0 = jnp.zeros((tq, 1), F32)
    acc0 = jnp.zeros((tq, d), F32)
    carry, acc = tile(j_diag, carry0, acc0, True)

    def body(i, state):
        return tile(j_diag - 1 - i, state[0], state[1], False)

    carry, acc = lax.fori_loop(0, j_diag, body, (carry, acc))
    o_ref[...] = acc.astype(o_ref.dtype)


def _stick_breaking(proj, batch, seq, n_heads):
    d, tq, tk = SB_HEAD_DIM, SB_Q_TILE, min(SB_K_TILE, seq)
    nq = seq // tq
    return pl.pallas_call(
        functools.partial(_sb_kernel, scale=d ** -0.5, tk=tk),
        grid=(batch, n_heads, nq),
        in_specs=[pl.BlockSpec((tq, d), lambda b, h, i: (b * nq + i, h)),
                  pl.BlockSpec((seq, d), lambda b, h, i: (b, n_heads + h)),
                  pl.BlockSpec((seq, d), lambda b, h, i: (b, 2 * n_heads + h))],
        out_specs=pl.BlockSpec((tq, d), lambda b, h, i: (b * nq + i, h)),
        out_shape=jax.ShapeDtypeStruct((batch * seq, n_heads * d), BF16),
        compiler_params=_params("parallel", "parallel", "arbitrary"),
        name="stick_breaking",
    )(proj, proj, proj)


def _mem_attn_kernel(q_ref, k_ref, v_ref, o_ref, *, n_heads):
    hd = q_ref.shape[1] // n_heads
    for h in range(n_heads):
        qb = q_ref[:, h * hd:(h + 1) * hd].astype(BF16)
        s = _dot_nt(qb, k_ref[:, h * hd:(h + 1) * hd]) * (hd ** -0.5)
        e = jnp.exp(s - jnp.max(s, axis=-1, keepdims=True))
        p = e / jnp.sum(e, axis=-1, keepdims=True)
        o_ref[:, h * hd:(h + 1) * hd] = _dot(p.astype(BF16), v_ref[:, h * hd:(h + 1) * hd]).astype(o_ref.dtype)


def _memory_attention(proj, mq_block, mem_kv, batch, seq, rows=512):
    w = mem_kv.shape[1] // 2
    mtok = mem_kv.shape[0] // batch
    rows = min(rows, seq)
    nt = seq // rows
    return pl.pallas_call(
        functools.partial(_mem_attn_kernel, n_heads=MEM_HEADS),
        grid=(batch, nt),
        in_specs=[pl.BlockSpec((rows, w), lambda b, i: (b * nt + i, mq_block)),
                  pl.BlockSpec((mtok, w), lambda b, i: (b, 0)),
                  pl.BlockSpec((mtok, w), lambda b, i: (b, 1))],
        out_specs=pl.BlockSpec((rows, w), lambda b, i: (b * nt + i, 0)),
        out_shape=jax.ShapeDtypeStruct((batch * seq, w), BF16),
        compiler_params=_params("parallel", "arbitrary"),
        name="memory_attention",
    )(proj, mem_kv, mem_kv)


def _ffn_half_step(h, norm_g, w_in, w_out):
    xn = _rmsnorm(h, norm_g, BF16)
    act = _swiglu_in(xn, w_in.astype(BF16))
    return _matmul_residual(act, w_out.astype(BF16), h, 0.5)


def kernel(x, mem, norm_ffn1, ffn1_w_in, ffn1_w_out, norm_mix, a_w_in, a_b_igate, a_b_fgate, a_head_gain, b_w_in, mem_norm, w_mem_kv, w_out, norm_ffn2, ffn2_w_in, ffn2_w_out, norm_final):
    batch, seq, d = x.shape
    mtok = mem.shape[1]
    depth = norm_ffn1.shape[0]
    mem_w = w_mem_kv.shape[1] // 2
    tok_w = w_out.shape[1] - mem_w
    a_heads = tok_w // MLSTM_HEAD_V
    b_heads = tok_w // SB_HEAD_DIM
    a_qk = a_heads * MLSTM_HEAD_QK

    mem_n = _rmsnorm(mem.reshape(batch * mtok, d), mem_norm, BF16)
    mem_kv = _matmul(mem_n, w_mem_kv.astype(BF16), BF16)

    h = x.reshape(batch * seq, d)
    for i in range(depth):
        h = _ffn_half_step(h, norm_ffn1[i], ffn1_w_in[i], ffn1_w_out[i])
        xn = _rmsnorm(h, norm_mix[i], BF16)
        j = i // 2
        if i % 2 == 0:
            w = a_w_in[j]
            n_main = 2 * a_qk + 2 * tok_w
            w_main = jnp.concatenate([w[:, :n_main], w[:, n_main + 2 * a_heads:]], axis=1).astype(BF16)
            proj = _matmul(xn, w_main, F32)
            gcol, grow = _mlstm_gates(xn, w[:, n_main:n_main + 2 * a_heads], a_b_igate[j], a_b_fgate[j], MLSTM_TILE)
            mixed = _mlstm(proj, gcol, grow, a_head_gain[j], batch, seq, a_heads)
            mq_block = n_main // mem_w
        else:
            proj = _matmul(xn, b_w_in[j].astype(BF16), F32)
            mixed = _stick_breaking(proj, batch, seq, b_heads)
            mq_block = (3 * tok_w) // mem_w
        mem_out = _memory_attention(proj, mq_block, mem_kv, batch, seq)
        h = _out_proj_residual(mixed, mem_out, w_out[i].astype(BF16), h)
        h = _ffn_half_step(h, norm_ffn2[i], ffn2_w_in[i], ffn2_w_out[i])
    return _rmsnorm(h, norm_final, F32).reshape(batch, seq, d)
```

```python
import functools

import jax
import jax.numpy as jnp
from jax import lax
from jax.experimental import pallas as pl
from jax.experimental.pallas import tpu as pltpu

F32 = jnp.float32
BF16 = jnp.bfloat16

RMS_EPS = 1e-6
GATE_SOFT_CAP = 15.0
MEM_HEADS = 4
MLSTM_HEAD_V = 512
MLSTM_HEAD_QK = 256
SB_HEAD_DIM = 128

V7X_VMEM_LIMIT_BYTES = 56 * 1024 * 1024
MM_ROWS = 2048
MLSTM_TILE = 128
SB_Q_TILE = 128
SB_K_TILE = 256
SB_HEAD_GROUP = 4
GATE_COLS = 128
GATE_ROWS = 16


def _params(*sem):
    return pltpu.CompilerParams(dimension_semantics=sem, vmem_limit_bytes=V7X_VMEM_LIMIT_BYTES)


def _dot(a, b):
    return jnp.dot(a, b, preferred_element_type=F32)


def _dot_nt(a, b):
    return lax.dot_general(a, b, (((1,), (1,)), ((), ())), preferred_element_type=F32)


def _dot_tn(a, b):
    return lax.dot_general(a, b, (((0,), (0,)), ((), ())), preferred_element_type=F32)


def _sigmoid(x):
    return 1.0 / (1.0 + jnp.exp(-x))


def _log_sigmoid(x):
    return jnp.minimum(x, 0.0) - jnp.log1p(jnp.exp(-jnp.abs(x)))


def _split3(x):
    hi = x.astype(BF16)
    r1 = x - hi.astype(F32)
    mid = r1.astype(BF16)
    lo = (r1 - mid.astype(F32)).astype(BF16)
    return hi, mid, lo


def _rmsnorm_kernel(x_ref, g_ref, o_ref):
    x = x_ref[...]
    ms = jnp.mean(x * x, axis=-1, keepdims=True)
    o_ref[...] = ((x * lax.rsqrt(ms + RMS_EPS)) * g_ref[...]).astype(o_ref.dtype)


def _rmsnorm(x, g, out_dtype, rows=256):
    m, d = x.shape
    rows = min(rows, m)
    return pl.pallas_call(
        _rmsnorm_kernel,
        grid=(m // rows,),
        in_specs=[pl.BlockSpec((rows, d), lambda i: (i, 0)),
                  pl.BlockSpec((1, d), lambda i: (0, 0))],
        out_specs=pl.BlockSpec((rows, d), lambda i: (i, 0)),
        out_shape=jax.ShapeDtypeStruct((m, d), out_dtype),
        compiler_params=_params("parallel"),
        name="rmsnorm",
    )(x, g.reshape(1, d))


def _act_spec(tm, k):
    return pl.BlockSpec((tm, k), lambda i, j: (i, 0), pipeline_mode=pl.Buffered(1))


def _w_spec(k, tn, layer, col_block0=0, row_block=0):
    return pl.BlockSpec((None, k, tn), lambda i, j: (layer, row_block, col_block0 + j))


def _mm_kernel(a_ref, w_ref, o_ref):
    o_ref[...] = _dot(a_ref[...], w_ref[...].astype(BF16)).astype(o_ref.dtype)


def _matmul(a, w, layer, col0, ncols, out_dtype, tn=512):
    m, k = a.shape
    tm = min(MM_ROWS, m)
    assert col0 % tn == 0 and ncols % tn == 0
    return pl.pallas_call(
        _mm_kernel,
        grid=(m // tm, ncols // tn),
        in_specs=[_act_spec(tm, k), _w_spec(k, tn, layer, col0 // tn)],
        out_specs=pl.BlockSpec((tm, tn), lambda i, j: (i, j)),
        out_shape=jax.ShapeDtypeStruct((m, ncols), out_dtype),
        compiler_params=_params("parallel", "arbitrary"),
        name="matmul",
    )(a, w)


def _swiglu_in_kernel(a_ref, wg_ref, wu_ref, o_ref):
    a = a_ref[...]
    g = _dot(a, wg_ref[...].astype(BF16))
    u = _dot(a, wu_ref[...].astype(BF16))
    o_ref[...] = ((g * _sigmoid(g)) * u).astype(o_ref.dtype)


def _swiglu_in(a, w_in, layer, tn=256):
    m, k = a.shape
    f = w_in.shape[2] // 2
    tm = min(MM_ROWS, m)
    nblk = f // tn
    return pl.pallas_call(
        _swiglu_in_kernel,
        grid=(m // tm, nblk),
        in_specs=[_act_spec(tm, k), _w_spec(k, tn, layer), _w_spec(k, tn, layer, nblk)],
        out_specs=pl.BlockSpec((tm, tn), lambda i, j: (i, j)),
        out_shape=jax.ShapeDtypeStruct((m, f), BF16),
        compiler_params=_params("parallel", "arbitrary"),
        name="swiglu_in",
    )(a, w_in, w_in)


def _mm_residual_kernel(a_ref, w_ref, r_ref, o_ref, *, scale):
    o_ref[...] = r_ref[...] + scale * _dot(a_ref[...], w_ref[...].astype(BF16))


def _matmul_residual(a, w, layer, res, scale, tm=1024, tn=512):
    m, k = a.shape
    n = w.shape[2]
    tm = min(tm, m)
    return pl.pallas_call(
        functools.partial(_mm_residual_kernel, scale=scale),
        grid=(m // tm, n // tn),
        in_specs=[_act_spec(tm, k), _w_spec(k, tn, layer),
                  pl.BlockSpec((tm, tn), lambda i, j: (i, j))],
        out_specs=pl.BlockSpec((tm, tn), lambda i, j: (i, j)),
        out_shape=jax.ShapeDtypeStruct((m, n), F32),
        compiler_params=_params("parallel", "arbitrary"),
        name="matmul_residual",
    )(a, w, res)


def _mm2_residual_kernel(a1_ref, a2_ref, w1_ref, w2_ref, r_ref, o_ref):
    acc = _dot(a1_ref[...], w1_ref[...].astype(BF16)) + _dot(a2_ref[...], w2_ref[...].astype(BF16))
    o_ref[...] = r_ref[...] + acc


def _out_proj_residual(a1, a2, w, layer, res, tn=256):
    m, k1 = a1.shape
    k2 = a2.shape[1]
    n = w.shape[2]
    assert k1 % k2 == 0 and w.shape[1] == k1 + k2
    tm = min(MM_ROWS, m)
    return pl.pallas_call(
        _mm2_residual_kernel,
        grid=(m // tm, n // tn),
        in_specs=[_act_spec(tm, k1), _act_spec(tm, k2),
                  _w_spec(k1, tn, layer), _w_spec(k2, tn, layer, row_block=k1 // k2),
                  pl.BlockSpec((tm, tn), lambda i, j: (i, j))],
        out_specs=pl.BlockSpec((tm, tn), lambda i, j: (i, j)),
        out_shape=jax.ShapeDtypeStruct((m, n), F32),
        compiler_params=_params("parallel", "arbitrary"),
        name="out_proj_residual",
    )(a1, a2, w, w, res)


def _gates_kernel(xn_ref, wg_ref, wgt_ref, brow_ref, bcol_ref, col_ref, row_ref, *, n_heads, tile):
    x = xn_ref[...]
    rows = x.shape[0]
    g_col = _dot(x, wg_ref[...]) + brow_ref[...]
    g_row = _dot_nt(wgt_ref[...], x) + bcol_ref[...]

    def transform(t, is_forget):
        capped = GATE_SOFT_CAP * jnp.tanh(t / GATE_SOFT_CAP)
        return jnp.where(is_forget, _log_sigmoid(capped), capped)

    lane = lax.broadcasted_iota(jnp.int32, g_col.shape, 1)
    sub = lax.broadcasted_iota(jnp.int32, g_row.shape, 0)
    t_col = transform(g_col, lane >= n_heads)
    t_row = transform(g_row, sub >= n_heads)

    r = lax.broadcasted_iota(jnp.int32, (tile, tile), 0)
    c = lax.broadcasted_iota(jnp.int32, (tile, tile), 1)
    lower = (c <= r).astype(BF16)
    upper = (r <= c).astype(BF16)
    forget_lane = lax.broadcasted_iota(jnp.int32, (tile, g_col.shape[1]), 1) >= n_heads
    forget_sub = lax.broadcasted_iota(jnp.int32, (g_row.shape[0], tile), 0) >= n_heads

    for t0 in range(0, rows, tile):
        blk = t_col[t0:t0 + tile, :]
        hi, mid, lo = _split3(blk)
        cs = _dot(lower, hi) + _dot(lower, mid) + _dot(lower, lo)
        col_ref[t0:t0 + tile, :] = jnp.where(forget_lane, cs, blk)

        blk = t_row[:, t0:t0 + tile]
        hi, mid, lo = _split3(blk)
        cs = _dot(hi, upper) + _dot(mid, upper) + _dot(lo, upper)
        row_ref[:, t0:t0 + tile] = jnp.where(forget_sub, cs, blk)


def _mlstm_gates(xn, w_gates, b_ig, b_fg, tile, rows=512):
    m, d = xn.shape
    n_heads = b_ig.shape[0]
    rows = min(rows, m)
    wg = jnp.zeros((d, GATE_COLS), BF16).at[:, :2 * n_heads].set(w_gates.astype(BF16))
    wgt = jnp.zeros((GATE_ROWS, d), BF16).at[:2 * n_heads, :].set(w_gates.T.astype(BF16))
    bias = jnp.concatenate([b_ig, b_fg]).astype(F32)
    brow = jnp.zeros((1, GATE_COLS), F32).at[0, :2 * n_heads].set(bias)
    bcol = jnp.zeros((GATE_ROWS, 1), F32).at[:2 * n_heads, 0].set(bias)
    return pl.pallas_call(
        functools.partial(_gates_kernel, n_heads=n_heads, tile=tile),
        grid=(m // rows,),
        in_specs=[pl.BlockSpec((rows, d), lambda i: (i, 0)),
                  pl.BlockSpec((d, GATE_COLS), lambda i: (0, 0)),
                  pl.BlockSpec((GATE_ROWS, d), lambda i: (0, 0)),
                  pl.BlockSpec((1, GATE_COLS), lambda i: (0, 0)),
                  pl.BlockSpec((GATE_ROWS, 1), lambda i: (0, 0))],
        out_specs=[pl.BlockSpec((rows, GATE_COLS), lambda i: (i, 0)),
                   pl.BlockSpec((GATE_ROWS, rows), lambda i: (0, i))],
        out_shape=[jax.ShapeDtypeStruct((m, GATE_COLS), F32),
                   jax.ShapeDtypeStruct((GATE_ROWS, m), F32)],
        compiler_params=_params("parallel"),
        name="mlstm_gates",
    )(xn, wg, wgt, brow, bcol)


def _mlstm_kernel(q_ref, k_ref, v_ref, o_ref, gcol_ref, grow_ref, gain_ref, out_ref,
                  c_ref, n_ref, m_ref, *, n_heads, dk, dv):
    @pl.when(pl.program_id(1) == 0)
    def _():
        c_ref[...] = jnp.zeros_like(c_ref)
        n_ref[...] = jnp.zeros_like(n_ref)
        m_ref[...] = jnp.zeros_like(m_ref)

    t = q_ref.shape[0]
    r = lax.broadcasted_iota(jnp.int32, (t, t), 0)
    c = lax.broadcasted_iota(jnp.int32, (t, t), 1)
    causal = c <= r
    gcol = gcol_ref[...]
    grow = grow_ref[...]

    for h in range(n_heads):
        qb = q_ref[:, h * dk:(h + 1) * dk]
        kb = k_ref[:, h * dk:(h + 1) * dk]
        vb = v_ref[:, h * dv:(h + 1) * dv]
        q = qb.astype(F32) * (dk ** -0.5)
        qb = q.astype(BF16)
        k = kb.astype(F32)
        v = vb.astype(F32)
        ig_col = gcol[:, h:h + 1]
        b_col = gcol[:, n_heads + h:n_heads + h + 1]
        ig_row = grow[h:h + 1, :]
        b_row = grow[n_heads + h:n_heads + h + 1, :]
        c_prev = c_ref[h]
        n_prev = n_ref[h]
        m_prev = m_ref[h]

        d_log = jnp.where(causal, (b_col - b_row) + ig_row, -jnp.inf)
        inter_log = b_col + m_prev
        m_t = jnp.maximum(inter_log, jnp.max(d_log, axis=1, keepdims=True))
        d_w = jnp.exp(d_log - m_t)
        inter_w = jnp.exp(inter_log - m_t)
        s = _dot_nt(qb, kb) * d_w
        num = _dot(s.astype(BF16), vb) + inter_w * _dot(qb, c_prev.astype(BF16))
        den = jnp.sum(s, axis=1, keepdims=True) + inter_w * jnp.sum(q * n_prev, axis=1, keepdims=True)
        hh = num / jnp.maximum(jnp.abs(den), jnp.exp(-m_t))

        b_last = b_col[t - 1:t, :]
        w_log = (b_last - b_col) + ig_col
        m_new = jnp.maximum(b_last + m_prev, jnp.max(w_log, axis=0, keepdims=True))
        decay = jnp.exp((b_last + m_prev) - m_new)
        w = jnp.exp(w_log - m_new)
        c_ref[h] = decay * c_prev + _dot_tn(kb, (w * v).astype(BF16))
        n_ref[h] = decay * n_prev + jnp.sum(w * k, axis=0, keepdims=True)
        m_ref[h] = m_new

        ms = jnp.mean(hh * hh, axis=-1, keepdims=True)
        y = (hh * lax.rsqrt(ms + RMS_EPS)) * gain_ref[:, h * dv:(h + 1) * dv]
        o = o_ref[:, h * dv:(h + 1) * dv]
        out_ref[:, h * dv:(h + 1) * dv] = (_sigmoid(o) * y).astype(out_ref.dtype)


def _mlstm(qkv, o_gate, gcol, grow, head_gain, batch, seq, n_heads):
    dk, dv, t = MLSTM_HEAD_QK, MLSTM_HEAD_V, MLSTM_TILE
    wqk, wv = n_heads * dk, n_heads * dv
    assert 2 * wqk == wv
    nt = seq // t
    row = lambda b, i: b * nt + i
    return pl.pallas_call(
        functools.partial(_mlstm_kernel, n_heads=n_heads, dk=dk, dv=dv),
        grid=(batch, nt),
        in_specs=[pl.BlockSpec((t, wqk), lambda b, i: (row(b, i), 0)),
                  pl.BlockSpec((t, wqk), lambda b, i: (row(b, i), 1)),
                  pl.BlockSpec((t, wv), lambda b, i: (row(b, i), 1)),
                  pl.BlockSpec((t, wv), lambda b, i: (row(b, i), 0)),
                  pl.BlockSpec((t, GATE_COLS), lambda b, i: (row(b, i), 0)),
                  pl.BlockSpec((GATE_ROWS, t), lambda b, i: (0, row(b, i))),
                  pl.BlockSpec((1, wv), lambda b, i: (0, 0))],
        out_specs=pl.BlockSpec((t, wv), lambda b, i: (row(b, i), 0)),
        out_shape=jax.ShapeDtypeStruct((batch * seq, wv), BF16),
        scratch_shapes=[pltpu.VMEM((n_heads, dk, dv), F32),
                        pltpu.VMEM((n_heads, 1, dk), F32),
                        pltpu.VMEM((n_heads, 1, 1), F32)],
        compiler_params=_params("parallel", "arbitrary"),
        name="mlstm",
    )(qkv, qkv, qkv, o_gate, gcol, grow, head_gain.reshape(1, wv))


def _sb_kernel(q_ref, k_ref, v_ref, o_ref, acc_ref, *, scale, tk, d):
    tq = q_ref.shape[0]
    heads = q_ref.shape[1] // d
    qi = pl.program_id(2)
    r = lax.broadcasted_iota(jnp.int32, (tk, tk), 0)
    c = lax.broadcasted_iota(jnp.int32, (tk, tk), 1)
    strictly_later = (r > c).astype(BF16)

    def tile(j, carries, diagonal):
        k0 = pl.multiple_of(j * tk, tk)
        if diagonal:
            t_pos = qi * tq + lax.broadcasted_iota(jnp.int32, (tq, tk), 0)
            s_pos = j * tk + lax.broadcasted_iota(jnp.int32, (tq, tk), 1)
            causal = s_pos < t_pos
        col = [slice(g * d, (g + 1) * d) for g in range(heads)]
        zs = [_dot_nt(q_ref[:, col[g]], k_ref[pl.ds(k0, tk), col[g]]) * scale for g in range(heads)]
        log_betas, log_keeps, splits = [], [], []
        for z in zs:
            log_beta = jnp.minimum(z, 0.0) - jnp.log(1.0 + jnp.exp(-jnp.abs(z)))
            log_keep = log_beta - z
            if diagonal:
                log_keep = jnp.where(causal, log_keep, 0.0)
            hi = log_keep.astype(BF16)
            lo = (log_keep - hi.astype(F32)).astype(BF16)
            log_betas.append(log_beta)
            log_keeps.append(log_keep)
            splits.append((hi, lo))
        laters = [_dot(hi, strictly_later) + _dot(lo, strictly_later) for hi, lo in splits]
        probs = []
        for g in range(heads):
            a = jnp.exp(log_betas[g] + (laters[g] + carries[g]))
            if diagonal:
                a = jnp.where(causal, a, 0.0)
            probs.append(a.astype(BF16))
        pv = jnp.concatenate([_dot(probs[g], v_ref[pl.ds(k0, tk), col[g]]) for g in range(heads)], axis=1)
        acc_ref[...] = pv if diagonal else acc_ref[...] + pv
        return tuple(carries[g] + jnp.sum(log_keeps[g], axis=1, keepdims=True) for g in range(heads))

    j_diag = (qi * tq + tq - 1) // tk
    carries = tile(j_diag, tuple(jnp.zeros((tq, 1), F32) for _ in range(heads)), True)
    lax.fori_loop(0, j_diag, lambda i, cs: tile(j_diag - 1 - i, cs, False), carries)
    o_ref[...] = acc_ref[...].astype(o_ref.dtype)


def _stick_breaking(proj, batch, seq, n_heads):
    d, tq, tk, grp = SB_HEAD_DIM, SB_Q_TILE, min(SB_K_TILE, seq), SB_HEAD_GROUP
    nq = seq // tq
    ng = n_heads // grp
    return pl.pallas_call(
        functools.partial(_sb_kernel, scale=d ** -0.5, tk=tk, d=d),
        grid=(batch, ng, nq),
        in_specs=[pl.BlockSpec((tq, grp * d), lambda b, h, i: (b * nq + i, h)),
                  pl.BlockSpec((seq, grp * d), lambda b, h, i: (b, ng + h)),
                  pl.BlockSpec((seq, grp * d), lambda b, h, i: (b, 2 * ng + h))],
        out_specs=pl.BlockSpec((tq, grp * d), lambda b, h, i: (b * nq + i, h)),
        out_shape=jax.ShapeDtypeStruct((batch * seq, n_heads * d), BF16),
        scratch_shapes=[pltpu.VMEM((tq, grp * d), F32)],
        compiler_params=_params("parallel", "parallel", "arbitrary"),
        name="stick_breaking",
    )(proj, proj, proj)


def _mem_attn_kernel(q_ref, k_ref, v_ref, o_ref, *, n_heads):
    hd = q_ref.shape[1] // n_heads
    for h in range(n_heads):
        cols = slice(h * hd, (h + 1) * hd)
        s = _dot_nt(q_ref[:, cols], k_ref[:, cols]) * (hd ** -0.5)
        e = jnp.exp(s - jnp.max(s, axis=-1, keepdims=True))
        p = e / jnp.sum(e, axis=-1, keepdims=True)
        o_ref[:, cols] = _dot(p.astype(BF16), v_ref[:, cols]).astype(o_ref.dtype)


def _memory_attention(proj, mq_block, mem_kv, batch, seq, rows=512):
    w = mem_kv.shape[1] // 2
    mtok = mem_kv.shape[0] // batch
    rows = min(rows, seq)
    nt = seq // rows
    return pl.pallas_call(
        functools.partial(_mem_attn_kernel, n_heads=MEM_HEADS),
        grid=(batch, nt),
        in_specs=[pl.BlockSpec((rows, w), lambda b, i: (b * nt + i, mq_block)),
                  pl.BlockSpec((mtok, w), lambda b, i: (b, 0)),
                  pl.BlockSpec((mtok, w), lambda b, i: (b, 1))],
        out_specs=pl.BlockSpec((rows, w), lambda b, i: (b * nt + i, 0)),
        out_shape=jax.ShapeDtypeStruct((batch * seq, w), BF16),
        compiler_params=_params("parallel", "arbitrary"),
        name="memory_attention",
    )(proj, mem_kv, mem_kv)


def _ffn_half_step(h, norm_g, w_in, w_out, layer):
    xn = _rmsnorm(h, norm_g, BF16)
    act = _swiglu_in(xn, w_in, layer)
    return _matmul_residual(act, w_out, layer, h, 0.5)


def kernel(x, mem, norm_ffn1, ffn1_w_in, ffn1_w_out, norm_mix, a_w_in, a_b_igate, a_b_fgate, a_head_gain, b_w_in, mem_norm, w_mem_kv, w_out, norm_ffn2, ffn2_w_in, ffn2_w_out, norm_final):
    batch, seq, d = x.shape
    mtok = mem.shape[1]
    depth = norm_ffn1.shape[0]
    mem_w = w_mem_kv.shape[1] // 2
    tok_w = w_out.shape[2] - mem_w
    a_heads = tok_w // MLSTM_HEAD_V
    b_heads = tok_w // SB_HEAD_DIM
    a_qkv = 2 * a_heads * MLSTM_HEAD_QK + tok_w

    mem_n = _rmsnorm(mem.reshape(batch * mtok, d), mem_norm, BF16)
    mem_kv = _matmul(mem_n, w_mem_kv[None], 0, 0, 2 * mem_w, BF16)

    h = x.reshape(batch * seq, d)
    for i in range(depth):
        h = _ffn_half_step(h, norm_ffn1[i], ffn1_w_in, ffn1_w_out, i)
        xn = _rmsnorm(h, norm_mix[i], BF16)
        j = i // 2
        if i % 2 == 0:
            qkv = _matmul(xn, a_w_in, j, 0, a_qkv, BF16)
            o_gate = _matmul(xn, a_w_in, j, a_qkv, tok_w, F32)
            gate_col0 = a_qkv + tok_w
            w_gates = a_w_in[j, :, gate_col0:gate_col0 + 2 * a_heads]
            w_mq = a_w_in[j, :, gate_col0 + 2 * a_heads:]
            mq = _matmul(xn, w_mq[None], 0, 0, mem_w, BF16)
            gcol, grow = _mlstm_gates(xn, w_gates, a_b_igate[j], a_b_fgate[j], MLSTM_TILE)
            mixed = _mlstm(qkv, o_gate, gcol, grow, a_head_gain[j], batch, seq, a_heads)
            mem_out = _memory_attention(mq, 0, mem_kv, batch, seq)
        else:
            proj = _matmul(xn, b_w_in, j, 0, 3 * tok_w + mem_w, BF16)
            mixed = _stick_breaking(proj, batch, seq, b_heads)
            mem_out = _memory_attention(proj, (3 * tok_w) // mem_w, mem_kv, batch, seq)
        h = _out_proj_residual(mixed, mem_out, w_out, i, h)
        h = _ffn_half_step(h, norm_ffn2[i], ffn2_w_in, ffn2_w_out, i)
    return _rmsnorm(h, norm_final, F32).reshape(batch, seq, d)
```

```python
import functools

import jax
import jax.numpy as jnp
from jax import lax
from jax.experimental import pallas as pl
from jax.experimental.pallas import tpu as pltpu

F32 = jnp.float32
BF16 = jnp.bfloat16

RMS_EPS = 1e-6
GATE_SOFT_CAP = 15.0
MEM_HEADS = 4
MLSTM_HEAD_V = 512
MLSTM_HEAD_QK = 256
SB_HEAD_DIM = 128

V7X_VMEM_LIMIT_BYTES = 56 * 1024 * 1024
MM_ROWS = 2048
MLSTM_TILE = 128
SB_Q_TILE = 256
SB_K_TILE = 256
SB_HEAD_GROUP = 8
SB_EXP_UNDERFLOW = -104.5
A_TAIL_TILE = 384
GATE_COLS = 128
GATE_ROWS = 16


def _params(*sem):
    return pltpu.CompilerParams(dimension_semantics=sem, vmem_limit_bytes=V7X_VMEM_LIMIT_BYTES)


def _dot(a, b):
    return jnp.dot(a, b, preferred_element_type=F32)


def _dot_nt(a, b):
    return lax.dot_general(a, b, (((1,), (1,)), ((), ())), preferred_element_type=F32)


def _dot_tn(a, b):
    return lax.dot_general(a, b, (((0,), (0,)), ((), ())), preferred_element_type=F32)


def _sigmoid(x):
    return 1.0 / (1.0 + jnp.exp(-x))


def _log_sigmoid(x):
    return jnp.minimum(x, 0.0) - jnp.log1p(jnp.exp(-jnp.abs(x)))


def _split3(x):
    hi = x.astype(BF16)
    r1 = x - hi.astype(F32)
    mid = r1.astype(BF16)
    lo = (r1 - mid.astype(F32)).astype(BF16)
    return hi, mid, lo


def _rmsnorm_kernel(x_ref, g_ref, o_ref):
    x = x_ref[...]
    ms = jnp.mean(x * x, axis=-1, keepdims=True)
    o_ref[...] = ((x * lax.rsqrt(ms + RMS_EPS)) * g_ref[...]).astype(o_ref.dtype)


def _rmsnorm(x, g, out_dtype, rows=256):
    m, d = x.shape
    rows = min(rows, m)
    return pl.pallas_call(
        _rmsnorm_kernel,
        grid=(m // rows,),
        in_specs=[pl.BlockSpec((rows, d), lambda i: (i, 0)),
                  pl.BlockSpec((1, d), lambda i: (0, 0))],
        out_specs=pl.BlockSpec((rows, d), lambda i: (i, 0)),
        out_shape=jax.ShapeDtypeStruct((m, d), out_dtype),
        compiler_params=_params("parallel"),
        name="rmsnorm",
    )(x, g.reshape(1, d))


def _act_spec(tm, k, resident=True):
    if resident:
        return pl.BlockSpec((tm, k), lambda i, j: (i, 0), pipeline_mode=pl.Buffered(1))
    return pl.BlockSpec((tm, k), lambda i, j: (i, 0))


def _w_spec(k, tn, layer, col_block0=0, row_block=0):
    return pl.BlockSpec((None, k, tn), lambda i, j: (layer, row_block, col_block0 + j))


def _mm_kernel(a_ref, w_ref, o_ref):
    o_ref[...] = _dot(a_ref[...], w_ref[...].astype(BF16)).astype(o_ref.dtype)


def _matmul(a, w, layer, col0, ncols, out_dtype, tm=MM_ROWS, tn=512, resident=True):
    m, k = a.shape
    tm = min(tm, m)
    assert col0 % tn == 0 and ncols % tn == 0
    return pl.pallas_call(
        _mm_kernel,
        grid=(m // tm, ncols // tn),
        in_specs=[_act_spec(tm, k, resident), _w_spec(k, tn, layer, col0 // tn)],
        out_specs=pl.BlockSpec((tm, tn), lambda i, j: (i, j)),
        out_shape=jax.ShapeDtypeStruct((m, ncols), out_dtype),
        compiler_params=_params("parallel", "arbitrary"),
        name="matmul",
    )(a, w)


def _swiglu_in_kernel(a_ref, wg_ref, wu_ref, o_ref):
    a = a_ref[...]
    g = _dot(a, wg_ref[...].astype(BF16))
    u = _dot(a, wu_ref[...].astype(BF16))
    o_ref[...] = ((g * _sigmoid(g)) * u).astype(o_ref.dtype)


def _swiglu_in(a, w_in, layer, tm=MM_ROWS, tn=256, resident=True):
    m, k = a.shape
    f = w_in.shape[2] // 2
    tm = min(tm, m)
    nblk = f // tn
    return pl.pallas_call(
        _swiglu_in_kernel,
        grid=(m // tm, nblk),
        in_specs=[_act_spec(tm, k, resident), _w_spec(k, tn, layer), _w_spec(k, tn, layer, nblk)],
        out_specs=pl.BlockSpec((tm, tn), lambda i, j: (i, j)),
        out_shape=jax.ShapeDtypeStruct((m, f), BF16),
        compiler_params=_params("parallel", "arbitrary"),
        name="swiglu_in",
    )(a, w_in, w_in)


def _mm_residual_kernel(a_ref, w_ref, r_ref, o_ref, *, scale):
    o_ref[...] = r_ref[...] + scale * _dot(a_ref[...], w_ref[...].astype(BF16))


def _matmul_residual(a, w, layer, res, scale, tm=1024, tn=256, resident=False):
    m, k = a.shape
    n = w.shape[2]
    tm = min(tm, m)
    return pl.pallas_call(
        functools.partial(_mm_residual_kernel, scale=scale),
        grid=(m // tm, n // tn),
        in_specs=[_act_spec(tm, k, resident), _w_spec(k, tn, layer),
                  pl.BlockSpec((tm, tn), lambda i, j: (i, j))],
        out_specs=pl.BlockSpec((tm, tn), lambda i, j: (i, j)),
        out_shape=jax.ShapeDtypeStruct((m, n), F32),
        compiler_params=_params("parallel", "arbitrary"),
        name="matmul_residual",
    )(a, w, res)


def _mm2_residual_kernel(a1_ref, a2_ref, w1_ref, w2_ref, r_ref, o_ref):
    acc = _dot(a1_ref[...], w1_ref[...].astype(BF16)) + _dot(a2_ref[...], w2_ref[...].astype(BF16))
    o_ref[...] = r_ref[...] + acc


def _out_proj_residual(a1, a2, w, layer, res, tm=1024, tn=512, resident=False):
    m, k1 = a1.shape
    k2 = a2.shape[1]
    n = w.shape[2]
    assert k1 % k2 == 0 and w.shape[1] == k1 + k2
    tm = min(tm, m)
    return pl.pallas_call(
        _mm2_residual_kernel,
        grid=(m // tm, n // tn),
        in_specs=[_act_spec(tm, k1, resident), _act_spec(tm, k2, resident),
                  _w_spec(k1, tn, layer), _w_spec(k2, tn, layer, row_block=k1 // k2),
                  pl.BlockSpec((tm, tn), lambda i, j: (i, j))],
        out_specs=pl.BlockSpec((tm, tn), lambda i, j: (i, j)),
        out_shape=jax.ShapeDtypeStruct((m, n), F32),
        compiler_params=_params("parallel", "arbitrary"),
        name="out_proj_residual",
    )(a1, a2, w, w, res)


def _gates_kernel(p_ref, brow_ref, bcol_ref, col_ref, row_ref, *, n_heads, tile):
    p = p_ref[...]
    rows = p.shape[0]
    g_col = p + brow_ref[...]
    g_row = p.T[:GATE_ROWS, :] + bcol_ref[...]

    def transform(t, is_forget):
        capped = GATE_SOFT_CAP * jnp.tanh(t / GATE_SOFT_CAP)
        return jnp.where(is_forget, _log_sigmoid(capped), capped)

    lane = lax.broadcasted_iota(jnp.int32, g_col.shape, 1)
    sub = lax.broadcasted_iota(jnp.int32, g_row.shape, 0)
    t_col = transform(g_col, lane >= n_heads)
    t_row = transform(g_row, sub >= n_heads)

    r = lax.broadcasted_iota(jnp.int32, (tile, tile), 0)
    c = lax.broadcasted_iota(jnp.int32, (tile, tile), 1)
    lower = (c <= r).astype(BF16)
    upper = (r <= c).astype(BF16)
    forget_lane = lax.broadcasted_iota(jnp.int32, (tile, g_col.shape[1]), 1) >= n_heads
    forget_sub = lax.broadcasted_iota(jnp.int32, (g_row.shape[0], tile), 0) >= n_heads

    for t0 in range(0, rows, tile):
        blk = t_col[t0:t0 + tile, :]
        hi, mid, lo = _split3(blk)
        cs = _dot(lower, hi) + _dot(lower, mid) + _dot(lower, lo)
        col_ref[t0:t0 + tile, :] = jnp.where(forget_lane, cs, blk)

        blk = t_row[:, t0:t0 + tile]
        hi, mid, lo = _split3(blk)
        cs = _dot(hi, upper) + _dot(mid, upper) + _dot(lo, upper)
        row_ref[:, t0:t0 + tile] = jnp.where(forget_sub, cs, blk)


def _mlstm_gates(tail, b_ig, b_fg, tile, rows=512):
    m = tail.shape[0]
    n_heads = b_ig.shape[0]
    rows = min(rows, m)
    bias = jnp.concatenate([b_ig, b_fg]).astype(F32)
    brow = jnp.zeros((1, GATE_COLS), F32).at[0, :2 * n_heads].set(bias)
    bcol = jnp.zeros((GATE_ROWS, 1), F32).at[:2 * n_heads, 0].set(bias)
    return pl.pallas_call(
        functools.partial(_gates_kernel, n_heads=n_heads, tile=tile),
        grid=(m // rows,),
        in_specs=[pl.BlockSpec((rows, GATE_COLS), lambda i: (i, 0)),
                  pl.BlockSpec((1, GATE_COLS), lambda i: (0, 0)),
                  pl.BlockSpec((GATE_ROWS, 1), lambda i: (0, 0))],
        out_specs=[pl.BlockSpec((rows, GATE_COLS), lambda i: (i, 0)),
                   pl.BlockSpec((GATE_ROWS, rows), lambda i: (0, i))],
        out_shape=[jax.ShapeDtypeStruct((m, GATE_COLS), F32),
                   jax.ShapeDtypeStruct((GATE_ROWS, m), F32)],
        compiler_params=_params("parallel"),
        name="mlstm_gates",
    )(tail, brow, bcol)


def _mlstm_kernel(q_ref, k_ref, v_ref, o_ref, gcol_ref, grow_ref, gain_ref, out_ref,
                  c_ref, n_ref, m_ref, *, n_heads, dk, dv):
    @pl.when(pl.program_id(1) == 0)
    def _():
        c_ref[...] = jnp.zeros_like(c_ref)
        n_ref[...] = jnp.zeros_like(n_ref)
        m_ref[...] = jnp.zeros_like(m_ref)

    t = q_ref.shape[0]
    r = lax.broadcasted_iota(jnp.int32, (t, t), 0)
    c = lax.broadcasted_iota(jnp.int32, (t, t), 1)
    causal = c <= r
    gcol = gcol_ref[...]
    grow = grow_ref[...]

    for h in range(n_heads):
        qb = q_ref[:, h * dk:(h + 1) * dk]
        kb = k_ref[:, h * dk:(h + 1) * dk]
        vb = v_ref[:, h * dv:(h + 1) * dv]
        q = qb.astype(F32) * (dk ** -0.5)
        qb = q.astype(BF16)
        k = kb.astype(F32)
        v = vb.astype(F32)
        ig_col = gcol[:, h:h + 1]
        b_col = gcol[:, n_heads + h:n_heads + h + 1]
        ig_row = grow[h:h + 1, :]
        b_row = grow[n_heads + h:n_heads + h + 1, :]
        c_prev = c_ref[h]
        n_prev = n_ref[h]
        m_prev = m_ref[h]

        d_log = jnp.where(causal, (b_col - b_row) + ig_row, -jnp.inf)
        inter_log = b_col + m_prev
        m_t = jnp.maximum(inter_log, jnp.max(d_log, axis=1, keepdims=True))
        d_w = jnp.exp(d_log - m_t)
        inter_w = jnp.exp(inter_log - m_t)
        s = _dot_nt(qb, kb) * d_w
        num = _dot(s.astype(BF16), vb) + inter_w * _dot(qb, c_prev.astype(BF16))
        den = jnp.sum(s, axis=1, keepdims=True) + inter_w * jnp.sum(q * n_prev, axis=1, keepdims=True)
        hh = num / jnp.maximum(jnp.abs(den), jnp.exp(-m_t))

        b_last = b_col[t - 1:t, :]
        w_log = (b_last - b_col) + ig_col
        m_new = jnp.maximum(b_last + m_prev, jnp.max(w_log, axis=0, keepdims=True))
        decay = jnp.exp((b_last + m_prev) - m_new)
        w = jnp.exp(w_log - m_new)
        c_ref[h] = decay * c_prev + _dot_tn(kb, (w * v).astype(BF16))
        n_ref[h] = decay * n_prev + jnp.sum(w * k, axis=0, keepdims=True)
        m_ref[h] = m_new

        ms = jnp.mean(hh * hh, axis=-1, keepdims=True)
        y = (hh * lax.rsqrt(ms + RMS_EPS)) * gain_ref[:, h * dv:(h + 1) * dv]
        o = o_ref[:, h * dv:(h + 1) * dv]
        out_ref[:, h * dv:(h + 1) * dv] = (_sigmoid(o) * y).astype(out_ref.dtype)


def _mlstm(qkv, o_gate, gcol, grow, head_gain, batch, seq, n_heads):
    dk, dv, t = MLSTM_HEAD_QK, MLSTM_HEAD_V, MLSTM_TILE
    wqk, wv = n_heads * dk, n_heads * dv
    assert 2 * wqk == wv
    nt = seq // t
    row = lambda b, i: b * nt + i
    return pl.pallas_call(
        functools.partial(_mlstm_kernel, n_heads=n_heads, dk=dk, dv=dv),
        grid=(batch, nt),
        in_specs=[pl.BlockSpec((t, wqk), lambda b, i: (row(b, i), 0)),
                  pl.BlockSpec((t, wqk), lambda b, i: (row(b, i), 1)),
                  pl.BlockSpec((t, wv), lambda b, i: (row(b, i), 1)),
                  pl.BlockSpec((t, wv), lambda b, i: (row(b, i), 0)),
                  pl.BlockSpec((t, GATE_COLS), lambda b, i: (row(b, i), 0)),
                  pl.BlockSpec((GATE_ROWS, t), lambda b, i: (0, row(b, i))),
                  pl.BlockSpec((1, wv), lambda b, i: (0, 0))],
        out_specs=pl.BlockSpec((t, wv), lambda b, i: (row(b, i), 0)),
        out_shape=jax.ShapeDtypeStruct((batch * seq, wv), BF16),
        scratch_shapes=[pltpu.VMEM((n_heads, dk, dv), F32),
                        pltpu.VMEM((n_heads, 1, dk), F32),
                        pltpu.VMEM((n_heads, 1, 1), F32)],
        compiler_params=_params("parallel", "arbitrary"),
        name="mlstm",
    )(qkv, qkv, qkv, o_gate, gcol, grow, head_gain.reshape(1, wv))


def _sb_kernel(q_ref, k_ref, v_ref, o_ref, acc_ref, *, scale, tk, d):
    tq = q_ref.shape[0]
    heads = q_ref.shape[1] // d
    qi = pl.program_id(2)
    r = lax.broadcasted_iota(jnp.int32, (tk, tk), 0)
    c = lax.broadcasted_iota(jnp.int32, (tk, tk), 1)
    strictly_later = (r > c).astype(BF16)

    def tile(j, carries, diagonal):
        k0 = pl.multiple_of(j * tk, tk)
        if diagonal:
            t_pos = qi * tq + lax.broadcasted_iota(jnp.int32, (tq, tk), 0)
            s_pos = j * tk + lax.broadcasted_iota(jnp.int32, (tq, tk), 1)
            causal = s_pos < t_pos
        col = [slice(g * d, (g + 1) * d) for g in range(heads)]
        zs = [_dot_nt(q_ref[:, col[g]], k_ref[pl.ds(k0, tk), col[g]]) * scale for g in range(heads)]
        log_betas, log_keeps, splits = [], [], []
        for z in zs:
            log_beta = jnp.minimum(z, 0.0) - jnp.log(1.0 + jnp.exp(-jnp.abs(z)))
            log_keep = log_beta - z
            if diagonal:
                log_keep = jnp.where(causal, log_keep, 0.0)
            hi = log_keep.astype(BF16)
            lo = (log_keep - hi.astype(F32)).astype(BF16)
            log_betas.append(log_beta)
            log_keeps.append(log_keep)
            splits.append((hi, lo))
        laters = [_dot(hi, strictly_later) + _dot(lo, strictly_later) for hi, lo in splits]
        probs = []
        for g in range(heads):
            a = jnp.exp(log_betas[g] + (laters[g] + carries[g]))
            if diagonal:
                a = jnp.where(causal, a, 0.0)
            probs.append(a.astype(BF16))
        pv = jnp.concatenate([_dot(probs[g], v_ref[pl.ds(k0, tk), col[g]]) for g in range(heads)], axis=1)
        acc_ref[...] = pv if diagonal else acc_ref[...] + pv
        return tuple(carries[g] + jnp.sum(log_keeps[g], axis=1, keepdims=True) for g in range(heads))

    def alive(carries):
        top = functools.reduce(jnp.maximum, carries)
        return (jnp.max(top) > SB_EXP_UNDERFLOW).astype(jnp.int32)

    def body(state):
        i, _, carries = state
        carries = tile(j_diag - 1 - i, carries, False)
        return i + 1, alive(carries), carries

    j_diag = (qi * tq + tq - 1) // tk
    carries = tile(j_diag, tuple(jnp.zeros((tq, 1), F32) for _ in range(heads)), True)
    lax.while_loop(lambda s: (s[0] < j_diag) & (s[1] > 0), body, (0, alive(carries), carries))
    o_ref[...] = acc_ref[...].astype(o_ref.dtype)


def _stick_breaking(proj, batch, seq, n_heads, tq=SB_Q_TILE, tk=SB_K_TILE, grp=SB_HEAD_GROUP):
    d, tk = SB_HEAD_DIM, min(tk, seq)
    assert tk % tq == 0, "only one key tile may cross the causal diagonal"
    nq = seq // tq
    ng = n_heads // grp
    return pl.pallas_call(
        functools.partial(_sb_kernel, scale=d ** -0.5, tk=tk, d=d),
        grid=(batch, ng, nq),
        in_specs=[pl.BlockSpec((tq, grp * d), lambda b, h, i: (b * nq + i, h)),
                  pl.BlockSpec((seq, grp * d), lambda b, h, i: (b, ng + h)),
                  pl.BlockSpec((seq, grp * d), lambda b, h, i: (b, 2 * ng + h))],
        out_specs=pl.BlockSpec((tq, grp * d), lambda b, h, i: (b * nq + i, h)),
        out_shape=jax.ShapeDtypeStruct((batch * seq, n_heads * d), BF16),
        scratch_shapes=[pltpu.VMEM((tq, grp * d), F32)],
        compiler_params=_params("parallel", "parallel", "arbitrary"),
        name="stick_breaking",
    )(proj, proj, proj)


def _mem_attn_kernel(q_ref, k_ref, v_ref, o_ref, *, n_heads):
    hd = q_ref.shape[1] // n_heads
    for h in range(n_heads):
        cols = slice(h * hd, (h + 1) * hd)
        s = _dot_nt(q_ref[:, cols], k_ref[:, cols]) * (hd ** -0.5)
        e = jnp.exp(s - jnp.max(s, axis=-1, keepdims=True))
        p = e / jnp.sum(e, axis=-1, keepdims=True)
        o_ref[:, cols] = _dot(p.astype(BF16), v_ref[:, cols]).astype(o_ref.dtype)


def _memory_attention(proj, mq_block, mem_kv, batch, seq, rows=512):
    w = mem_kv.shape[1] // 2
    mtok = mem_kv.shape[0] // batch
    rows = min(rows, seq)
    nt = seq // rows
    return pl.pallas_call(
        functools.partial(_mem_attn_kernel, n_heads=MEM_HEADS),
        grid=(batch, nt),
        in_specs=[pl.BlockSpec((rows, w), lambda b, i: (b * nt + i, mq_block)),
                  pl.BlockSpec((mtok, w), lambda b, i: (b, 0)),
                  pl.BlockSpec((mtok, w), lambda b, i: (b, 1))],
        out_specs=pl.BlockSpec((rows, w), lambda b, i: (b * nt + i, 0)),
        out_shape=jax.ShapeDtypeStruct((batch * seq, w), BF16),
        compiler_params=_params("parallel", "arbitrary"),
        name="memory_attention",
    )(proj, mem_kv, mem_kv)


def _ffn_half_step(h, norm_g, w_in, w_out, layer):
    xn = _rmsnorm(h, norm_g, BF16)
    act = _swiglu_in(xn, w_in, layer)
    return _matmul_residual(act, w_out, layer, h, 0.5)


def kernel(x, mem, norm_ffn1, ffn1_w_in, ffn1_w_out, norm_mix, a_w_in, a_b_igate, a_b_fgate, a_head_gain, b_w_in, mem_norm, w_mem_kv, w_out, norm_ffn2, ffn2_w_in, ffn2_w_out, norm_final):
    batch, seq, d = x.shape
    mtok = mem.shape[1]
    depth = norm_ffn1.shape[0]
    mem_w = w_mem_kv.shape[1] // 2
    tok_w = w_out.shape[2] - mem_w
    a_heads = tok_w // MLSTM_HEAD_V
    b_heads = tok_w // SB_HEAD_DIM
    a_qkv = 2 * a_heads * MLSTM_HEAD_QK + tok_w

    mem_n = _rmsnorm(mem.reshape(batch * mtok, d), mem_norm, BF16)
    mem_kv = _matmul(mem_n, w_mem_kv[None], 0, 0, 2 * mem_w, BF16)

    h = x.reshape(batch * seq, d)
    for i in range(depth):
        h = _ffn_half_step(h, norm_ffn1[i], ffn1_w_in, ffn1_w_out, i)
        xn = _rmsnorm(h, norm_mix[i], BF16)
        j = i // 2
        if i % 2 == 0:
            qkv = _matmul(xn, a_w_in, j, 0, a_qkv, BF16)
            o_gate = _matmul(xn, a_w_in, j, a_qkv, tok_w, F32)
            tail_w = -(-(2 * a_heads + mem_w) // A_TAIL_TILE) * A_TAIL_TILE
            tail = _matmul(xn, a_w_in, j, a_qkv + tok_w, tail_w, F32, tn=A_TAIL_TILE)
            mq = tail[:, 2 * a_heads:2 * a_heads + mem_w].astype(BF16)
            gcol, grow = _mlstm_gates(tail, a_b_igate[j], a_b_fgate[j], MLSTM_TILE)
            mixed = _mlstm(qkv, o_gate, gcol, grow, a_head_gain[j], batch, seq, a_heads)
            mem_out = _memory_attention(mq, 0, mem_kv, batch, seq)
        else:
            proj = _matmul(xn, b_w_in, j, 0, 3 * tok_w + mem_w, BF16)
            mixed = _stick_breaking(proj, batch, seq, b_heads)
            mem_out = _memory_attention(proj, (3 * tok_w) // mem_w, mem_kv, batch, seq)
        h = _out_proj_residual(mixed, mem_out, w_out, i, h)
        h = _ffn_half_step(h, norm_ffn2[i], ffn2_w_in, ffn2_w_out, i)
    return _rmsnorm(h, norm_final, F32).reshape(batch, seq, d)
```

```python
import functools

import jax
import jax.numpy as jnp
from jax import lax
from jax.experimental import pallas as pl
from jax.experimental.pallas import tpu as pltpu

F32 = jnp.float32
BF16 = jnp.bfloat16

RMS_EPS = 1e-6
GATE_SOFT_CAP = 15.0
MEM_HEADS = 4
MLSTM_HEAD_V = 512
MLSTM_HEAD_QK = 256
SB_HEAD_DIM = 128

V7X_VMEM_LIMIT_BYTES = 56 * 1024 * 1024
LANES = 128
MM_ROWS = 2048
MLSTM_TILE = 128
SB_Q_TILE = 256
SB_K_TILE = 256
SB_HEAD_GROUP = 8
SB_EXP_UNDERFLOW = -104.5
A_TAIL_TILE = 384
GATE_COLS = 128
GATE_ROWS = 16


def _params(*sem):
    return pltpu.CompilerParams(dimension_semantics=sem, vmem_limit_bytes=V7X_VMEM_LIMIT_BYTES)


def _dot(a, b):
    return jnp.dot(a, b, preferred_element_type=F32)


def _dot_nt(a, b):
    return lax.dot_general(a, b, (((1,), (1,)), ((), ())), preferred_element_type=F32)


def _dot_tn(a, b):
    return lax.dot_general(a, b, (((0,), (0,)), ((), ())), preferred_element_type=F32)


def _sigmoid(x):
    return 1.0 / (1.0 + jnp.exp(-x))


def _log_sigmoid(x):
    return jnp.minimum(x, 0.0) - jnp.log1p(jnp.exp(-jnp.abs(x)))


def _split3(x):
    hi = x.astype(BF16)
    r1 = x - hi.astype(F32)
    mid = r1.astype(BF16)
    lo = (r1 - mid.astype(F32)).astype(BF16)
    return hi, mid, lo


def _rmsnorm_kernel(x_ref, g_ref, o_ref):
    x = x_ref[...]
    ms = jnp.mean(x * x, axis=-1, keepdims=True)
    o_ref[...] = ((x * lax.rsqrt(ms + RMS_EPS)) * g_ref[...]).astype(o_ref.dtype)


def _rmsnorm(x, g, out_dtype, rows=256):
    m, d = x.shape
    rows = min(rows, m)
    return pl.pallas_call(
        _rmsnorm_kernel,
        grid=(m // rows,),
        in_specs=[pl.BlockSpec((rows, d), lambda i: (i, 0)),
                  pl.BlockSpec((1, d), lambda i: (0, 0))],
        out_specs=pl.BlockSpec((rows, d), lambda i: (i, 0)),
        out_shape=jax.ShapeDtypeStruct((m, d), out_dtype),
        compiler_params=_params("parallel"),
        name="rmsnorm",
    )(x, g.reshape(1, d))


def _act_spec(tm, k, resident=True):
    if resident:
        return pl.BlockSpec((tm, k), lambda i, j: (i, 0), pipeline_mode=pl.Buffered(1))
    return pl.BlockSpec((tm, k), lambda i, j: (i, 0))


def _w_spec(k, tn, layer, col_block0=0, row_block=0):
    return pl.BlockSpec((None, k, tn), lambda i, j: (layer, row_block, col_block0 + j))


def _scale_rows(x, rs_ref):
    return x * jnp.tile(rs_ref[...], (1, x.shape[1] // LANES))


def _fold_lanes(x):
    out = x[:, :LANES]
    for c in range(LANES, x.shape[1], LANES):
        out = out + x[:, c:c + LANES]
    return out


def _finish_row_scale(folded, width):
    total = jnp.sum(folded, axis=1, keepdims=True)
    return jnp.broadcast_to(lax.rsqrt(total / width + RMS_EPS), folded.shape)


def _emit_norm_inputs(h, g_ref, hg_ref, rs_ref, width):
    hg_ref[...] = (h * g_ref[...]).astype(hg_ref.dtype)
    part = _fold_lanes(h * h)
    j = pl.program_id(1)

    @pl.when(j == 0)
    def _():
        rs_ref[...] = part

    @pl.when(j > 0)
    def _():
        rs_ref[...] += part

    @pl.when(j == width // h.shape[1] - 1)
    def _():
        rs_ref[...] = _finish_row_scale(rs_ref[...], width)


def _norm_out_specs(tm, tn):
    return [pl.BlockSpec((tm, tn), lambda i, j: (i, j)), pl.BlockSpec((tm, LANES), lambda i, j: (i, 0))]


def _norm_out_shapes(m, n):
    return [jax.ShapeDtypeStruct((m, n), BF16), jax.ShapeDtypeStruct((m, LANES), F32)]


def _ssq_spec(tm):
    return pl.BlockSpec((tm, LANES), lambda i, j: (i, 0))


def _norm_prep_kernel(x_ref, g_ref, hg_ref, rs_ref):
    x = x_ref[...]
    hg_ref[...] = (x * g_ref[...]).astype(hg_ref.dtype)
    rs_ref[...] = _finish_row_scale(_fold_lanes(x * x), x.shape[1])


def _norm_prep(x, g, rows=256):
    m, d = x.shape
    return pl.pallas_call(
        _norm_prep_kernel,
        grid=(m // rows,),
        in_specs=[pl.BlockSpec((rows, d), lambda i: (i, 0)),
                  pl.BlockSpec((1, d), lambda i: (0, 0))],
        out_specs=[pl.BlockSpec((rows, d), lambda i: (i, 0)),
                   pl.BlockSpec((rows, LANES), lambda i: (i, 0))],
        out_shape=_norm_out_shapes(m, d),
        compiler_params=_params("parallel"),
        name="norm_prep",
    )(x, g.reshape(1, d))


def _mm_kernel(a_ref, w_ref, o_ref):
    o_ref[...] = _dot(a_ref[...], w_ref[...].astype(BF16)).astype(o_ref.dtype)


def _mm_normed_kernel(a_ref, rs_ref, w_ref, o_ref):
    o_ref[...] = _scale_rows(_dot(a_ref[...], w_ref[...].astype(BF16)), rs_ref).astype(o_ref.dtype)


def _matmul(a, w, layer, col0, ncols, out_dtype, ssq=None, tm=MM_ROWS, tn=512, resident=True):
    m, k = a.shape
    tm = min(tm, m)
    assert col0 % tn == 0 and ncols % tn == 0
    normed = ssq is not None
    return pl.pallas_call(
        _mm_normed_kernel if normed else _mm_kernel,
        grid=(m // tm, ncols // tn),
        in_specs=[_act_spec(tm, k, resident)] + ([_ssq_spec(tm)] if normed else [])
                 + [_w_spec(k, tn, layer, col0 // tn)],
        out_specs=pl.BlockSpec((tm, tn), lambda i, j: (i, j)),
        out_shape=jax.ShapeDtypeStruct((m, ncols), out_dtype),
        compiler_params=_params("parallel", "arbitrary"),
        name="matmul",
    )(*((a, ssq, w) if normed else (a, w)))


def _swiglu_in_kernel(a_ref, rs_ref, wg_ref, wu_ref, o_ref):
    a = a_ref[...]
    g = _scale_rows(_dot(a, wg_ref[...].astype(BF16)), rs_ref)
    u = _scale_rows(_dot(a, wu_ref[...].astype(BF16)), rs_ref)
    o_ref[...] = ((g * _sigmoid(g)) * u).astype(o_ref.dtype)


def _swiglu_in(a, ssq, w_in, layer, tm=MM_ROWS, tn=256, resident=True):
    m, k = a.shape
    f = w_in.shape[2] // 2
    tm = min(tm, m)
    nblk = f // tn
    return pl.pallas_call(
        _swiglu_in_kernel,
        grid=(m // tm, nblk),
        in_specs=[_act_spec(tm, k, resident), _ssq_spec(tm),
                  _w_spec(k, tn, layer), _w_spec(k, tn, layer, nblk)],
        out_specs=pl.BlockSpec((tm, tn), lambda i, j: (i, j)),
        out_shape=jax.ShapeDtypeStruct((m, f), BF16),
        compiler_params=_params("parallel", "arbitrary"),
        name="swiglu_in",
    )(a, ssq, w_in, w_in)


def _mm_residual_kernel(a_ref, w_ref, r_ref, *rest, scale, norm_width):
    h = r_ref[...] + scale * _dot(a_ref[...], w_ref[...].astype(BF16))
    if norm_width:
        g_ref, o_ref, hg_ref, rs_ref = rest
        _emit_norm_inputs(h, g_ref, hg_ref, rs_ref, norm_width)
    else:
        o_ref, = rest
    o_ref[...] = h


def _matmul_residual(a, w, layer, res, scale, next_gain=None, tm=1024, tn=256, resident=False):
    m, k = a.shape
    n = w.shape[2]
    tm = min(tm, m)
    emit = next_gain is not None
    tile = pl.BlockSpec((tm, tn), lambda i, j: (i, j))
    out = pl.pallas_call(
        functools.partial(_mm_residual_kernel, scale=scale, norm_width=n if emit else 0),
        grid=(m // tm, n // tn),
        in_specs=[_act_spec(tm, k, resident), _w_spec(k, tn, layer), tile]
                 + ([pl.BlockSpec((1, tn), lambda i, j: (0, j))] if emit else []),
        out_specs=[tile] + (_norm_out_specs(tm, tn) if emit else []),
        out_shape=[jax.ShapeDtypeStruct((m, n), F32)] + (_norm_out_shapes(m, n) if emit else []),
        compiler_params=_params("parallel", "arbitrary"),
        name="matmul_residual",
    )(*((a, w, res, next_gain.reshape(1, n)) if emit else (a, w, res)))
    return out if emit else out[0]


def _mm2_residual_kernel(a1_ref, a2_ref, w1_ref, w2_ref, r_ref, g_ref, o_ref, hg_ref, rs_ref, *, norm_width):
    acc = _dot(a1_ref[...], w1_ref[...].astype(BF16)) + _dot(a2_ref[...], w2_ref[...].astype(BF16))
    h = r_ref[...] + acc
    o_ref[...] = h
    _emit_norm_inputs(h, g_ref, hg_ref, rs_ref, norm_width)


def _out_proj_residual(a1, a2, w, layer, res, next_gain, tm=1024, tn=512, resident=False):
    m, k1 = a1.shape
    k2 = a2.shape[1]
    n = w.shape[2]
    assert k1 % k2 == 0 and w.shape[1] == k1 + k2
    tm = min(tm, m)
    tile = pl.BlockSpec((tm, tn), lambda i, j: (i, j))
    return pl.pallas_call(
        functools.partial(_mm2_residual_kernel, norm_width=n),
        grid=(m // tm, n // tn),
        in_specs=[_act_spec(tm, k1, resident), _act_spec(tm, k2, resident),
                  _w_spec(k1, tn, layer), _w_spec(k2, tn, layer, row_block=k1 // k2),
                  tile, pl.BlockSpec((1, tn), lambda i, j: (0, j))],
        out_specs=[tile] + _norm_out_specs(tm, tn),
        out_shape=[jax.ShapeDtypeStruct((m, n), F32)] + _norm_out_shapes(m, n),
        compiler_params=_params("parallel", "arbitrary"),
        name="out_proj_residual",
    )(a1, a2, w, w, res, next_gain.reshape(1, n))


def _gates_kernel(p_ref, brow_ref, bcol_ref, col_ref, row_ref, *, n_heads, tile):
    p = p_ref[...]
    rows = p.shape[0]
    g_col = p + brow_ref[...]
    g_row = p.T[:GATE_ROWS, :] + bcol_ref[...]

    def transform(t, is_forget):
        capped = GATE_SOFT_CAP * jnp.tanh(t / GATE_SOFT_CAP)
        return jnp.where(is_forget, _log_sigmoid(capped), capped)

    lane = lax.broadcasted_iota(jnp.int32, g_col.shape, 1)
    sub = lax.broadcasted_iota(jnp.int32, g_row.shape, 0)
    t_col = transform(g_col, lane >= n_heads)
    t_row = transform(g_row, sub >= n_heads)

    r = lax.broadcasted_iota(jnp.int32, (tile, tile), 0)
    c = lax.broadcasted_iota(jnp.int32, (tile, tile), 1)
    lower = (c <= r).astype(BF16)
    upper = (r <= c).astype(BF16)
    forget_lane = lax.broadcasted_iota(jnp.int32, (tile, g_col.shape[1]), 1) >= n_heads
    forget_sub = lax.broadcasted_iota(jnp.int32, (g_row.shape[0], tile), 0) >= n_heads

    for t0 in range(0, rows, tile):
        blk = t_col[t0:t0 + tile, :]
        hi, mid, lo = _split3(blk)
        cs = _dot(lower, hi) + _dot(lower, mid) + _dot(lower, lo)
        col_ref[t0:t0 + tile, :] = jnp.where(forget_lane, cs, blk)

        blk = t_row[:, t0:t0 + tile]
        hi, mid, lo = _split3(blk)
        cs = _dot(hi, upper) + _dot(mid, upper) + _dot(lo, upper)
        row_ref[:, t0:t0 + tile] = jnp.where(forget_sub, cs, blk)


def _mlstm_gates(tail, b_ig, b_fg, tile, rows=512):
    m = tail.shape[0]
    n_heads = b_ig.shape[0]
    rows = min(rows, m)
    bias = jnp.concatenate([b_ig, b_fg]).astype(F32)
    brow = jnp.zeros((1, GATE_COLS), F32).at[0, :2 * n_heads].set(bias)
    bcol = jnp.zeros((GATE_ROWS, 1), F32).at[:2 * n_heads, 0].set(bias)
    return pl.pallas_call(
        functools.partial(_gates_kernel, n_heads=n_heads, tile=tile),
        grid=(m // rows,),
        in_specs=[pl.BlockSpec((rows, GATE_COLS), lambda i: (i, 0)),
                  pl.BlockSpec((1, GATE_COLS), lambda i: (0, 0)),
                  pl.BlockSpec((GATE_ROWS, 1), lambda i: (0, 0))],
        out_specs=[pl.BlockSpec((rows, GATE_COLS), lambda i: (i, 0)),
                   pl.BlockSpec((GATE_ROWS, rows), lambda i: (0, i))],
        out_shape=[jax.ShapeDtypeStruct((m, GATE_COLS), F32),
                   jax.ShapeDtypeStruct((GATE_ROWS, m), F32)],
        compiler_params=_params("parallel"),
        name="mlstm_gates",
    )(tail, brow, bcol)


def _mlstm_kernel(q_ref, k_ref, v_ref, o_ref, gcol_ref, grow_ref, gain_ref, out_ref,
                  c_ref, n_ref, m_ref, *, n_heads, dk, dv):
    @pl.when(pl.program_id(1) == 0)
    def _():
        c_ref[...] = jnp.zeros_like(c_ref)
        n_ref[...] = jnp.zeros_like(n_ref)
        m_ref[...] = jnp.zeros_like(m_ref)

    t = q_ref.shape[0]
    r = lax.broadcasted_iota(jnp.int32, (t, t), 0)
    c = lax.broadcasted_iota(jnp.int32, (t, t), 1)
    causal = c <= r
    gcol = gcol_ref[...]
    grow = grow_ref[...]

    for h in range(n_heads):
        qb = q_ref[:, h * dk:(h + 1) * dk]
        kb = k_ref[:, h * dk:(h + 1) * dk]
        vb = v_ref[:, h * dv:(h + 1) * dv]
        q = qb.astype(F32) * (dk ** -0.5)
        qb = q.astype(BF16)
        k = kb.astype(F32)
        v = vb.astype(F32)
        ig_col = gcol[:, h:h + 1]
        b_col = gcol[:, n_heads + h:n_heads + h + 1]
        ig_row = grow[h:h + 1, :]
        b_row = grow[n_heads + h:n_heads + h + 1, :]
        c_prev = c_ref[h]
        n_prev = n_ref[h]
        m_prev = m_ref[h]

        d_log = jnp.where(causal, (b_col - b_row) + ig_row, -jnp.inf)
        inter_log = b_col + m_prev
        m_t = jnp.maximum(inter_log, jnp.max(d_log, axis=1, keepdims=True))
        d_w = jnp.exp(d_log - m_t)
        inter_w = jnp.exp(inter_log - m_t)
        s = _dot_nt(qb, kb) * d_w
        num = _dot(s.astype(BF16), vb) + inter_w * _dot(qb, c_prev.astype(BF16))
        den = jnp.sum(s, axis=1, keepdims=True) + inter_w * jnp.sum(q * n_prev, axis=1, keepdims=True)
        hh = num / jnp.maximum(jnp.abs(den), jnp.exp(-m_t))

        b_last = b_col[t - 1:t, :]
        w_log = (b_last - b_col) + ig_col
        m_new = jnp.maximum(b_last + m_prev, jnp.max(w_log, axis=0, keepdims=True))
        decay = jnp.exp((b_last + m_prev) - m_new)
        w = jnp.exp(w_log - m_new)
        c_ref[h] = decay * c_prev + _dot_tn(kb, (w * v).astype(BF16))
        n_ref[h] = decay * n_prev + jnp.sum(w * k, axis=0, keepdims=True)
        m_ref[h] = m_new

        ms = jnp.mean(hh * hh, axis=-1, keepdims=True)
        y = (hh * lax.rsqrt(ms + RMS_EPS)) * gain_ref[:, h * dv:(h + 1) * dv]
        o = o_ref[:, h * dv:(h + 1) * dv]
        out_ref[:, h * dv:(h + 1) * dv] = (_sigmoid(o) * y).astype(out_ref.dtype)


def _mlstm(qkv, o_gate, gcol, grow, head_gain, batch, seq, n_heads):
    dk, dv, t = MLSTM_HEAD_QK, MLSTM_HEAD_V, MLSTM_TILE
    wqk, wv = n_heads * dk, n_heads * dv
    assert 2 * wqk == wv
    nt = seq // t
    row = lambda b, i: b * nt + i
    return pl.pallas_call(
        functools.partial(_mlstm_kernel, n_heads=n_heads, dk=dk, dv=dv),
        grid=(batch, nt),
        in_specs=[pl.BlockSpec((t, wqk), lambda b, i: (row(b, i), 0)),
                  pl.BlockSpec((t, wqk), lambda b, i: (row(b, i), 1)),
                  pl.BlockSpec((t, wv), lambda b, i: (row(b, i), 1)),
                  pl.BlockSpec((t, wv), lambda b, i: (row(b, i), 0)),
                  pl.BlockSpec((t, GATE_COLS), lambda b, i: (row(b, i), 0)),
                  pl.BlockSpec((GATE_ROWS, t), lambda b, i: (0, row(b, i))),
                  pl.BlockSpec((1, wv), lambda b, i: (0, 0))],
        out_specs=pl.BlockSpec((t, wv), lambda b, i: (row(b, i), 0)),
        out_shape=jax.ShapeDtypeStruct((batch * seq, wv), BF16),
        scratch_shapes=[pltpu.VMEM((n_heads, dk, dv), F32),
                        pltpu.VMEM((n_heads, 1, dk), F32),
                        pltpu.VMEM((n_heads, 1, 1), F32)],
        compiler_params=_params("parallel", "arbitrary"),
        name="mlstm",
    )(qkv, qkv, qkv, o_gate, gcol, grow, head_gain.reshape(1, wv))


def _sb_kernel(q_ref, k_ref, v_ref, o_ref, acc_ref, *, scale, tk, d):
    tq = q_ref.shape[0]
    heads = q_ref.shape[1] // d
    qi = pl.program_id(2)
    r = lax.broadcasted_iota(jnp.int32, (tk, tk), 0)
    c = lax.broadcasted_iota(jnp.int32, (tk, tk), 1)
    strictly_later = (r > c).astype(BF16)

    def tile(j, carries, diagonal):
        k0 = pl.multiple_of(j * tk, tk)
        if diagonal:
            t_pos = qi * tq + lax.broadcasted_iota(jnp.int32, (tq, tk), 0)
            s_pos = j * tk + lax.broadcasted_iota(jnp.int32, (tq, tk), 1)
            causal = s_pos < t_pos
        col = [slice(g * d, (g + 1) * d) for g in range(heads)]
        zs = [_dot_nt(q_ref[:, col[g]], k_ref[pl.ds(k0, tk), col[g]]) * scale for g in range(heads)]
        log_betas, log_keeps, splits = [], [], []
        for z in zs:
            log_beta = jnp.minimum(z, 0.0) - jnp.log(1.0 + jnp.exp(-jnp.abs(z)))
            log_keep = log_beta - z
            if diagonal:
                log_keep = jnp.where(causal, log_keep, 0.0)
            hi = log_keep.astype(BF16)
            lo = (log_keep - hi.astype(F32)).astype(BF16)
            log_betas.append(log_beta)
            log_keeps.append(log_keep)
            splits.append((hi, lo))
        laters = [_dot(hi, strictly_later) + _dot(lo, strictly_later) for hi, lo in splits]
        probs = []
        for g in range(heads):
            a = jnp.exp(log_betas[g] + (laters[g] + carries[g]))
            if diagonal:
                a = jnp.where(causal, a, 0.0)
            probs.append(a.astype(BF16))
        pv = jnp.concatenate([_dot(probs[g], v_ref[pl.ds(k0, tk), col[g]]) for g in range(heads)], axis=1)
        acc_ref[...] = pv if diagonal else acc_ref[...] + pv
        return tuple(carries[g] + jnp.sum(log_keeps[g], axis=1, keepdims=True) for g in range(heads))

    def alive(carries):
        top = functools.reduce(jnp.maximum, carries)
        return (jnp.max(top) > SB_EXP_UNDERFLOW).astype(jnp.int32)

    def body(state):
        i, _, carries = state
        carries = tile(j_diag - 1 - i, carries, False)
        return i + 1, alive(carries), carries

    j_diag = (qi * tq + tq - 1) // tk
    carries = tile(j_diag, tuple(jnp.zeros((tq, 1), F32) for _ in range(heads)), True)
    lax.while_loop(lambda s: (s[0] < j_diag) & (s[1] > 0), body, (0, alive(carries), carries))
    o_ref[...] = acc_ref[...].astype(o_ref.dtype)


def _stick_breaking(proj, batch, seq, n_heads, tq=SB_Q_TILE, tk=SB_K_TILE, grp=SB_HEAD_GROUP):
    d, tk = SB_HEAD_DIM, min(tk, seq)
    assert tk % tq == 0, "only one key tile may cross the causal diagonal"
    assert n_heads % grp == 0 and seq % tk == 0
    nq = seq // tq
    ng = n_heads // grp
    return pl.pallas_call(
        functools.partial(_sb_kernel, scale=d ** -0.5, tk=tk, d=d),
        grid=(batch, ng, nq),
        in_specs=[pl.BlockSpec((tq, grp * d), lambda b, h, i: (b * nq + i, h)),
                  pl.BlockSpec((seq, grp * d), lambda b, h, i: (b, ng + h)),
                  pl.BlockSpec((seq, grp * d), lambda b, h, i: (b, 2 * ng + h))],
        out_specs=pl.BlockSpec((tq, grp * d), lambda b, h, i: (b * nq + i, h)),
        out_shape=jax.ShapeDtypeStruct((batch * seq, n_heads * d), BF16),
        scratch_shapes=[pltpu.VMEM((tq, grp * d), F32)],
        compiler_params=_params("parallel", "parallel", "arbitrary"),
        name="stick_breaking",
    )(proj, proj, proj)


def _mem_attn_kernel(q_ref, k_ref, v_ref, o_ref, *, n_heads):
    hd = q_ref.shape[1] // n_heads
    for h in range(n_heads):
        cols = slice(h * hd, (h + 1) * hd)
        s = _dot_nt(q_ref[:, cols], k_ref[:, cols]) * (hd ** -0.5)
        e = jnp.exp(s - jnp.max(s, axis=-1, keepdims=True))
        p = e / jnp.sum(e, axis=-1, keepdims=True)
        o_ref[:, cols] = _dot(p.astype(BF16), v_ref[:, cols]).astype(o_ref.dtype)


def _memory_attention(proj, mq_block, mem_kv, batch, seq, rows=512):
    w = mem_kv.shape[1] // 2
    mtok = mem_kv.shape[0] // batch
    rows = min(rows, seq)
    nt = seq // rows
    return pl.pallas_call(
        functools.partial(_mem_attn_kernel, n_heads=MEM_HEADS),
        grid=(batch, nt),
        in_specs=[pl.BlockSpec((rows, w), lambda b, i: (b * nt + i, mq_block)),
                  pl.BlockSpec((mtok, w), lambda b, i: (b, 0)),
                  pl.BlockSpec((mtok, w), lambda b, i: (b, 1))],
        out_specs=pl.BlockSpec((rows, w), lambda b, i: (b * nt + i, 0)),
        out_shape=jax.ShapeDtypeStruct((batch * seq, w), BF16),
        compiler_params=_params("parallel", "arbitrary"),
        name="memory_attention",
    )(proj, mem_kv, mem_kv)


def _ffn_half_step(h, hg, ssq, w_in, w_out, layer, next_gain):
    act = _swiglu_in(hg, ssq, w_in, layer)
    return _matmul_residual(act, w_out, layer, h, 0.5, next_gain)


def kernel(x, mem, norm_ffn1, ffn1_w_in, ffn1_w_out, norm_mix, a_w_in, a_b_igate, a_b_fgate, a_head_gain, b_w_in, mem_norm, w_mem_kv, w_out, norm_ffn2, ffn2_w_in, ffn2_w_out, norm_final):
    batch, seq, d = x.shape
    mtok = mem.shape[1]
    depth = norm_ffn1.shape[0]
    mem_w = w_mem_kv.shape[1] // 2
    tok_w = w_out.shape[2] - mem_w
    a_heads = tok_w // MLSTM_HEAD_V
    b_heads = tok_w // SB_HEAD_DIM
    a_qkv = 2 * a_heads * MLSTM_HEAD_QK + tok_w

    mem_n = _rmsnorm(mem.reshape(batch * mtok, d), mem_norm, BF16)
    mem_kv = _matmul(mem_n, w_mem_kv[None], 0, 0, 2 * mem_w, BF16)

    h = x.reshape(batch * seq, d)
    hg, ssq = _norm_prep(h, norm_ffn1[0])
    for i in range(depth):
        h, hg, ssq = _ffn_half_step(h, hg, ssq, ffn1_w_in, ffn1_w_out, i, norm_mix[i])
        j = i // 2
        if i % 2 == 0:
            qkv = _matmul(hg, a_w_in, j, 0, a_qkv, BF16, ssq)
            o_gate = _matmul(hg, a_w_in, j, a_qkv, tok_w, F32, ssq)
            tail_w = -(-(2 * a_heads + mem_w) // A_TAIL_TILE) * A_TAIL_TILE
            tail = _matmul(hg, a_w_in, j, a_qkv + tok_w, tail_w, F32, ssq, tn=A_TAIL_TILE)
            mq = tail[:, 2 * a_heads:2 * a_heads + mem_w].astype(BF16)
            gcol, grow = _mlstm_gates(tail, a_b_igate[j], a_b_fgate[j], MLSTM_TILE)
            mixed = _mlstm(qkv, o_gate, gcol, grow, a_head_gain[j], batch, seq, a_heads)
            mem_out = _memory_attention(mq, 0, mem_kv, batch, seq)
        else:
            proj = _matmul(hg, b_w_in, j, 0, 3 * tok_w + mem_w, BF16, ssq)
            mixed = _stick_breaking(proj, batch, seq, b_heads)
            mem_out = _memory_attention(proj, (3 * tok_w) // mem_w, mem_kv, batch, seq)
        h, hg, ssq = _out_proj_residual(mixed, mem_out, w_out, i, h, norm_ffn2[i])
        if i + 1 < depth:
            h, hg, ssq = _ffn_half_step(h, hg, ssq, ffn2_w_in, ffn2_w_out, i, norm_ffn1[i + 1])
        else:
            h = _ffn_half_step(h, hg, ssq, ffn2_w_in, ffn2_w_out, i, None)
    return _rmsnorm(h, norm_final, F32).reshape(batch, seq, d)
```

```python
import functools

import jax
import jax.numpy as jnp
from jax import lax
from jax.experimental import pallas as pl
from jax.experimental.pallas import tpu as pltpu

F32 = jnp.float32
BF16 = jnp.bfloat16

RMS_EPS = 1e-6
GATE_SOFT_CAP = 15.0
MEM_HEADS = 4
MLSTM_HEAD_V = 512
MLSTM_HEAD_QK = 256
SB_HEAD_DIM = 128

V7X_VMEM_LIMIT_BYTES = 56 * 1024 * 1024
LANES = 128
MM_ROWS = 2048
MLSTM_TILE = 256
SB_Q_TILE = 256
SB_K_TILE = 256
SB_HEAD_GROUP = 8
SB_EXP_UNDERFLOW = -104.5
A_TAIL_TILE = 384
GATE_COLS = 128
GATE_ROWS = 16


def _params(*sem):
    return pltpu.CompilerParams(dimension_semantics=sem, vmem_limit_bytes=V7X_VMEM_LIMIT_BYTES)


def _dot(a, b):
    return jnp.dot(a, b, preferred_element_type=F32)


def _dot_nt(a, b):
    return lax.dot_general(a, b, (((1,), (1,)), ((), ())), preferred_element_type=F32)


def _dot_tn(a, b):
    return lax.dot_general(a, b, (((0,), (0,)), ((), ())), preferred_element_type=F32)


def _sigmoid(x):
    return 1.0 / (1.0 + jnp.exp(-x))


def _log_sigmoid(x):
    return jnp.minimum(x, 0.0) - jnp.log1p(jnp.exp(-jnp.abs(x)))


def _split3(x):
    hi = x.astype(BF16)
    r1 = x - hi.astype(F32)
    mid = r1.astype(BF16)
    lo = (r1 - mid.astype(F32)).astype(BF16)
    return hi, mid, lo


def _rmsnorm_kernel(x_ref, g_ref, o_ref):
    x = x_ref[...]
    ms = jnp.mean(x * x, axis=-1, keepdims=True)
    o_ref[...] = ((x * lax.rsqrt(ms + RMS_EPS)) * g_ref[...]).astype(o_ref.dtype)


def _rmsnorm(x, g, out_dtype, rows=256):
    m, d = x.shape
    rows = min(rows, m)
    return pl.pallas_call(
        _rmsnorm_kernel,
        grid=(m // rows,),
        in_specs=[pl.BlockSpec((rows, d), lambda i: (i, 0)),
                  pl.BlockSpec((1, d), lambda i: (0, 0))],
        out_specs=pl.BlockSpec((rows, d), lambda i: (i, 0)),
        out_shape=jax.ShapeDtypeStruct((m, d), out_dtype),
        compiler_params=_params("parallel"),
        name="rmsnorm",
    )(x, g.reshape(1, d))


def _act_spec(tm, k, resident=True):
    if resident:
        return pl.BlockSpec((tm, k), lambda i, j: (i, 0), pipeline_mode=pl.Buffered(1))
    return pl.BlockSpec((tm, k), lambda i, j: (i, 0))


def _w_spec(k, tn, layer, col_block0=0, row_block=0):
    return pl.BlockSpec((None, k, tn), lambda i, j: (layer, row_block, col_block0 + j))


def _scale_rows(x, rs_ref):
    return x * jnp.tile(rs_ref[...], (1, x.shape[1] // LANES))


def _fold_lanes(x):
    out = x[:, :LANES]
    for c in range(LANES, x.shape[1], LANES):
        out = out + x[:, c:c + LANES]
    return out


def _finish_row_scale(folded, width):
    total = jnp.sum(folded, axis=1, keepdims=True)
    return jnp.broadcast_to(lax.rsqrt(total / width + RMS_EPS), folded.shape)


def _emit_norm_inputs(h, g_ref, hg_ref, rs_ref, width):
    hg_ref[...] = (h * g_ref[...]).astype(hg_ref.dtype)
    part = _fold_lanes(h * h)
    j = pl.program_id(1)

    @pl.when(j == 0)
    def _():
        rs_ref[...] = part

    @pl.when(j > 0)
    def _():
        rs_ref[...] += part

    @pl.when(j == width // h.shape[1] - 1)
    def _():
        rs_ref[...] = _finish_row_scale(rs_ref[...], width)


def _norm_out_specs(tm, tn):
    return [pl.BlockSpec((tm, tn), lambda i, j: (i, j)), pl.BlockSpec((tm, LANES), lambda i, j: (i, 0))]


def _norm_out_shapes(m, n):
    return [jax.ShapeDtypeStruct((m, n), BF16), jax.ShapeDtypeStruct((m, LANES), F32)]


def _ssq_spec(tm):
    return pl.BlockSpec((tm, LANES), lambda i, j: (i, 0))


def _norm_prep_kernel(x_ref, g_ref, hg_ref, rs_ref):
    x = x_ref[...]
    hg_ref[...] = (x * g_ref[...]).astype(hg_ref.dtype)
    rs_ref[...] = _finish_row_scale(_fold_lanes(x * x), x.shape[1])


def _norm_prep(x, g, rows=256):
    m, d = x.shape
    return pl.pallas_call(
        _norm_prep_kernel,
        grid=(m // rows,),
        in_specs=[pl.BlockSpec((rows, d), lambda i: (i, 0)),
                  pl.BlockSpec((1, d), lambda i: (0, 0))],
        out_specs=[pl.BlockSpec((rows, d), lambda i: (i, 0)),
                   pl.BlockSpec((rows, LANES), lambda i: (i, 0))],
        out_shape=_norm_out_shapes(m, d),
        compiler_params=_params("parallel"),
        name="norm_prep",
    )(x, g.reshape(1, d))


def _mm_kernel(a_ref, w_ref, o_ref):
    o_ref[...] = _dot(a_ref[...], w_ref[...].astype(BF16)).astype(o_ref.dtype)


def _mm_normed_kernel(a_ref, rs_ref, w_ref, o_ref):
    o_ref[...] = _scale_rows(_dot(a_ref[...], w_ref[...].astype(BF16)), rs_ref).astype(o_ref.dtype)


def _matmul(a, w, layer, col0, ncols, out_dtype, ssq=None, tm=MM_ROWS, tn=512, resident=True):
    m, k = a.shape
    tm = min(tm, m)
    assert col0 % tn == 0 and ncols % tn == 0
    normed = ssq is not None
    return pl.pallas_call(
        _mm_normed_kernel if normed else _mm_kernel,
        grid=(m // tm, ncols // tn),
        in_specs=[_act_spec(tm, k, resident)] + ([_ssq_spec(tm)] if normed else [])
                 + [_w_spec(k, tn, layer, col0 // tn)],
        out_specs=pl.BlockSpec((tm, tn), lambda i, j: (i, j)),
        out_shape=jax.ShapeDtypeStruct((m, ncols), out_dtype),
        compiler_params=_params("parallel", "arbitrary"),
        name="matmul",
    )(*((a, ssq, w) if normed else (a, w)))


def _swiglu_in_kernel(a_ref, rs_ref, wg_ref, wu_ref, o_ref):
    a = a_ref[...]
    half = wg_ref.shape[1] // 2
    halves = [slice(c * half, (c + 1) * half) for c in range(2)]
    prods = [(_dot(a, wg_ref[:, cols].astype(BF16)), _dot(a, wu_ref[:, cols].astype(BF16)))
             for cols in halves]
    for cols, (g, u) in zip(halves, prods):
        g = _scale_rows(g, rs_ref)
        u = _scale_rows(u, rs_ref)
        o_ref[:, cols] = ((g * _sigmoid(g)) * u).astype(o_ref.dtype)


def _swiglu_in(a, ssq, w_in, layer, tm=1024, tn=512, resident=True):
    m, k = a.shape
    f = w_in.shape[2] // 2
    tm = min(tm, m)
    nblk = f // tn
    return pl.pallas_call(
        _swiglu_in_kernel,
        grid=(m // tm, nblk),
        in_specs=[_act_spec(tm, k, resident), _ssq_spec(tm),
                  _w_spec(k, tn, layer), _w_spec(k, tn, layer, nblk)],
        out_specs=pl.BlockSpec((tm, tn), lambda i, j: (i, j)),
        out_shape=jax.ShapeDtypeStruct((m, f), BF16),
        compiler_params=_params("parallel", "arbitrary"),
        name="swiglu_in",
    )(a, ssq, w_in, w_in)


def _mm_residual_kernel(a_ref, w_ref, r_ref, *rest, scale, norm_width):
    h = r_ref[...] + scale * _dot(a_ref[...], w_ref[...].astype(BF16))
    if norm_width:
        g_ref, o_ref, hg_ref, rs_ref = rest
        _emit_norm_inputs(h, g_ref, hg_ref, rs_ref, norm_width)
    else:
        o_ref, = rest
    o_ref[...] = h


def _matmul_residual(a, w, layer, res, scale, next_gain=None, tm=1024, tn=256, resident=False):
    m, k = a.shape
    n = w.shape[2]
    tm = min(tm, m)
    emit = next_gain is not None
    tile = pl.BlockSpec((tm, tn), lambda i, j: (i, j))
    out = pl.pallas_call(
        functools.partial(_mm_residual_kernel, scale=scale, norm_width=n if emit else 0),
        grid=(m // tm, n // tn),
        in_specs=[_act_spec(tm, k, resident), _w_spec(k, tn, layer), tile]
                 + ([pl.BlockSpec((1, tn), lambda i, j: (0, j))] if emit else []),
        out_specs=[tile] + (_norm_out_specs(tm, tn) if emit else []),
        out_shape=[jax.ShapeDtypeStruct((m, n), F32)] + (_norm_out_shapes(m, n) if emit else []),
        compiler_params=_params("parallel", "arbitrary"),
        name="matmul_residual",
    )(*((a, w, res, next_gain.reshape(1, n)) if emit else (a, w, res)))
    return out if emit else out[0]


def _mm2_residual_kernel(a1_ref, a2_ref, w1_ref, w2_ref, r_ref, g_ref, o_ref, hg_ref, rs_ref, *, norm_width):
    half = w1_ref.shape[1] // 2
    halves = [slice(c * half, (c + 1) * half) for c in range(2)]
    a1, a2 = a1_ref[...], a2_ref[...]
    accs = [_dot(a1, w1_ref[:, cols].astype(BF16)) + _dot(a2, w2_ref[:, cols].astype(BF16))
            for cols in halves]
    h = jnp.concatenate([r_ref[:, cols] + acc for cols, acc in zip(halves, accs)], axis=1)
    o_ref[...] = h
    _emit_norm_inputs(h, g_ref, hg_ref, rs_ref, norm_width)


def _out_proj_residual(a1, a2, w, layer, res, next_gain, tm=1024, tn=512, resident=False):
    m, k1 = a1.shape
    k2 = a2.shape[1]
    n = w.shape[2]
    assert k1 % k2 == 0 and w.shape[1] == k1 + k2
    tm = min(tm, m)
    tile = pl.BlockSpec((tm, tn), lambda i, j: (i, j))
    return pl.pallas_call(
        functools.partial(_mm2_residual_kernel, norm_width=n),
        grid=(m // tm, n // tn),
        in_specs=[_act_spec(tm, k1, resident), _act_spec(tm, k2, resident),
                  _w_spec(k1, tn, layer), _w_spec(k2, tn, layer, row_block=k1 // k2),
                  tile, pl.BlockSpec((1, tn), lambda i, j: (0, j))],
        out_specs=[tile] + _norm_out_specs(tm, tn),
        out_shape=[jax.ShapeDtypeStruct((m, n), F32)] + _norm_out_shapes(m, n),
        compiler_params=_params("parallel", "arbitrary"),
        name="out_proj_residual",
    )(a1, a2, w, w, res, next_gain.reshape(1, n))


def _gates_kernel(p_ref, brow_ref, bcol_ref, col_ref, row_ref, *, n_heads, tile):
    p = p_ref[...]
    rows = p.shape[0]
    g_col = p + brow_ref[...]
    g_row = p.T[:GATE_ROWS, :] + bcol_ref[...]

    def transform(t, is_forget):
        capped = GATE_SOFT_CAP * jnp.tanh(t / GATE_SOFT_CAP)
        return jnp.where(is_forget, _log_sigmoid(capped), capped)

    lane = lax.broadcasted_iota(jnp.int32, g_col.shape, 1)
    sub = lax.broadcasted_iota(jnp.int32, g_row.shape, 0)
    t_col = transform(g_col, lane >= n_heads)
    t_row = transform(g_row, sub >= n_heads)

    r = lax.broadcasted_iota(jnp.int32, (tile, tile), 0)
    c = lax.broadcasted_iota(jnp.int32, (tile, tile), 1)
    lower = (c <= r).astype(BF16)
    upper = (r <= c).astype(BF16)
    forget_lane = lax.broadcasted_iota(jnp.int32, (tile, g_col.shape[1]), 1) >= n_heads
    forget_sub = lax.broadcasted_iota(jnp.int32, (g_row.shape[0], tile), 0) >= n_heads

    for t0 in range(0, rows, tile):
        blk = t_col[t0:t0 + tile, :]
        hi, mid, lo = _split3(blk)
        cs = _dot(lower, hi) + _dot(lower, mid) + _dot(lower, lo)
        col_ref[t0:t0 + tile, :] = jnp.where(forget_lane, cs, blk)

        blk = t_row[:, t0:t0 + tile]
        hi, mid, lo = _split3(blk)
        cs = _dot(hi, upper) + _dot(mid, upper) + _dot(lo, upper)
        row_ref[:, t0:t0 + tile] = jnp.where(forget_sub, cs, blk)


def _mlstm_gates(tail, b_ig, b_fg, tile, rows=512):
    m = tail.shape[0]
    n_heads = b_ig.shape[0]
    rows = min(rows, m)
    bias = jnp.concatenate([b_ig, b_fg]).astype(F32)
    brow = jnp.zeros((1, GATE_COLS), F32).at[0, :2 * n_heads].set(bias)
    bcol = jnp.zeros((GATE_ROWS, 1), F32).at[:2 * n_heads, 0].set(bias)
    return pl.pallas_call(
        functools.partial(_gates_kernel, n_heads=n_heads, tile=tile),
        grid=(m // rows,),
        in_specs=[pl.BlockSpec((rows, GATE_COLS), lambda i: (i, 0)),
                  pl.BlockSpec((1, GATE_COLS), lambda i: (0, 0)),
                  pl.BlockSpec((GATE_ROWS, 1), lambda i: (0, 0))],
        out_specs=[pl.BlockSpec((rows, GATE_COLS), lambda i: (i, 0)),
                   pl.BlockSpec((GATE_ROWS, rows), lambda i: (0, i))],
        out_shape=[jax.ShapeDtypeStruct((m, GATE_COLS), F32),
                   jax.ShapeDtypeStruct((GATE_ROWS, m), F32)],
        compiler_params=_params("parallel"),
        name="mlstm_gates",
    )(tail, brow, bcol)


def _mlstm_kernel(q_ref, k_ref, v_ref, o_ref, gcol_ref, grow_ref, gain_ref, out_ref,
                  c_ref, n_ref, m_ref, *, n_heads, dk, dv):
    @pl.when(pl.program_id(1) == 0)
    def _():
        c_ref[...] = jnp.zeros_like(c_ref)
        n_ref[...] = jnp.zeros_like(n_ref)
        m_ref[...] = jnp.zeros_like(m_ref)

    t = q_ref.shape[0]
    r = lax.broadcasted_iota(jnp.int32, (t, t), 0)
    c = lax.broadcasted_iota(jnp.int32, (t, t), 1)
    causal = c <= r
    gcol = gcol_ref[...]
    grow = grow_ref[...]

    for h in range(n_heads):
        qb = q_ref[:, h * dk:(h + 1) * dk]
        kb = k_ref[:, h * dk:(h + 1) * dk]
        vb = v_ref[:, h * dv:(h + 1) * dv]
        q = qb.astype(F32) * (dk ** -0.5)
        qb = q.astype(BF16)
        k = kb.astype(F32)
        v = vb.astype(F32)
        ig_col = gcol[:, h:h + 1]
        b_col = gcol[:, n_heads + h:n_heads + h + 1]
        ig_row = grow[h:h + 1, :]
        b_row = grow[n_heads + h:n_heads + h + 1, :]
        c_prev = c_ref[h]
        n_prev = n_ref[h]
        m_prev = m_ref[h]

        d_log = jnp.where(causal, (b_col - b_row) + ig_row, -jnp.inf)
        inter_log = b_col + m_prev
        m_t = jnp.maximum(inter_log, jnp.max(d_log, axis=1, keepdims=True))
        d_w = jnp.exp(d_log - m_t)
        inter_w = jnp.exp(inter_log - m_t)
        s = _dot_nt(qb, kb) * d_w
        num = _dot(s.astype(BF16), vb) + inter_w * _dot(qb, c_prev.astype(BF16))
        den = jnp.sum(s, axis=1, keepdims=True) + inter_w * jnp.sum(q * n_prev, axis=1, keepdims=True)
        hh = num / jnp.maximum(jnp.abs(den), jnp.exp(-m_t))

        b_last = b_col[t - 1:t, :]
        w_log = (b_last - b_col) + ig_col
        m_new = jnp.maximum(b_last + m_prev, jnp.max(w_log, axis=0, keepdims=True))
        decay = jnp.exp((b_last + m_prev) - m_new)
        w = jnp.exp(w_log - m_new)
        c_ref[h] = decay * c_prev + _dot_tn(kb, (w * v).astype(BF16))
        n_ref[h] = decay * n_prev + jnp.sum(w * k, axis=0, keepdims=True)
        m_ref[h] = m_new

        ms = jnp.mean(hh * hh, axis=-1, keepdims=True)
        y = (hh * lax.rsqrt(ms + RMS_EPS)) * gain_ref[:, h * dv:(h + 1) * dv]
        o = o_ref[:, h * dv:(h + 1) * dv]
        out_ref[:, h * dv:(h + 1) * dv] = (_sigmoid(o) * y).astype(out_ref.dtype)


def _mlstm(qkv, o_gate, gcol, grow, head_gain, batch, seq, n_heads):
    dk, dv, t = MLSTM_HEAD_QK, MLSTM_HEAD_V, MLSTM_TILE
    wqk, wv = n_heads * dk, n_heads * dv
    assert 2 * wqk == wv
    nt = seq // t
    row = lambda b, i: b * nt + i
    return pl.pallas_call(
        functools.partial(_mlstm_kernel, n_heads=n_heads, dk=dk, dv=dv),
        grid=(batch, nt),
        in_specs=[pl.BlockSpec((t, wqk), lambda b, i: (row(b, i), 0)),
                  pl.BlockSpec((t, wqk), lambda b, i: (row(b, i), 1)),
                  pl.BlockSpec((t, wv), lambda b, i: (row(b, i), 1)),
                  pl.BlockSpec((t, wv), lambda b, i: (row(b, i), 0)),
                  pl.BlockSpec((t, GATE_COLS), lambda b, i: (row(b, i), 0)),
                  pl.BlockSpec((GATE_ROWS, t), lambda b, i: (0, row(b, i))),
                  pl.BlockSpec((1, wv), lambda b, i: (0, 0))],
        out_specs=pl.BlockSpec((t, wv), lambda b, i: (row(b, i), 0)),
        out_shape=jax.ShapeDtypeStruct((batch * seq, wv), BF16),
        scratch_shapes=[pltpu.VMEM((n_heads, dk, dv), F32),
                        pltpu.VMEM((n_heads, 1, dk), F32),
                        pltpu.VMEM((n_heads, 1, 1), F32)],
        compiler_params=_params("parallel", "arbitrary"),
        name="mlstm",
    )(qkv, qkv, qkv, o_gate, gcol, grow, head_gain.reshape(1, wv))


def _sb_kernel(q_ref, k_ref, v_ref, o_ref, acc_ref, *, scale, tk, d):
    tq = q_ref.shape[0]
    heads = q_ref.shape[1] // d
    qi = pl.program_id(2)
    r = lax.broadcasted_iota(jnp.int32, (tk, tk), 0)
    c = lax.broadcasted_iota(jnp.int32, (tk, tk), 1)
    strictly_later = (r > c).astype(BF16)

    def tile(j, carries, diagonal):
        k0 = pl.multiple_of(j * tk, tk)
        if diagonal:
            t_pos = qi * tq + lax.broadcasted_iota(jnp.int32, (tq, tk), 0)
            s_pos = j * tk + lax.broadcasted_iota(jnp.int32, (tq, tk), 1)
            causal = s_pos < t_pos
        col = [slice(g * d, (g + 1) * d) for g in range(heads)]
        zs = [_dot_nt(q_ref[:, col[g]], k_ref[pl.ds(k0, tk), col[g]]) * scale for g in range(heads)]
        log_betas, log_keeps, splits = [], [], []
        for z in zs:
            log_beta = jnp.minimum(z, 0.0) - jnp.log(1.0 + jnp.exp(-jnp.abs(z)))
            log_keep = log_beta - z
            if diagonal:
                log_keep = jnp.where(causal, log_keep, 0.0)
            hi = log_keep.astype(BF16)
            lo = (log_keep - hi.astype(F32)).astype(BF16)
            log_betas.append(log_beta)
            log_keeps.append(log_keep)
            splits.append((hi, lo))
        laters = [_dot(hi, strictly_later) + _dot(lo, strictly_later) for hi, lo in splits]
        probs = []
        for g in range(heads):
            a = jnp.exp(log_betas[g] + (laters[g] + carries[g]))
            if diagonal:
                a = jnp.where(causal, a, 0.0)
            probs.append(a.astype(BF16))
        pv = jnp.concatenate([_dot(probs[g], v_ref[pl.ds(k0, tk), col[g]]) for g in range(heads)], axis=1)
        acc_ref[...] = pv if diagonal else acc_ref[...] + pv
        return tuple(carries[g] + jnp.sum(log_keeps[g], axis=1, keepdims=True) for g in range(heads))

    def alive(carries):
        top = functools.reduce(jnp.maximum, carries)
        return (jnp.max(top) > SB_EXP_UNDERFLOW).astype(jnp.int32)

    def body(state):
        i, _, carries = state
        carries = tile(j_diag - 1 - i, carries, False)
        return i + 1, alive(carries), carries

    j_diag = (qi * tq + tq - 1) // tk
    carries = tile(j_diag, tuple(jnp.zeros((tq, 1), F32) for _ in range(heads)), True)
    lax.while_loop(lambda s: (s[0] < j_diag) & (s[1] > 0), body, (0, alive(carries), carries))
    o_ref[...] = acc_ref[...].astype(o_ref.dtype)


def _stick_breaking(proj, batch, seq, n_heads, tq=SB_Q_TILE, tk=SB_K_TILE, grp=SB_HEAD_GROUP):
    d, tk = SB_HEAD_DIM, min(tk, seq)
    assert tk % tq == 0, "only one key tile may cross the causal diagonal"
    assert n_heads % grp == 0 and seq % tk == 0
    nq = seq // tq
    ng = n_heads // grp
    return pl.pallas_call(
        functools.partial(_sb_kernel, scale=d ** -0.5, tk=tk, d=d),
        grid=(batch, ng, nq),
        in_specs=[pl.BlockSpec((tq, grp * d), lambda b, h, i: (b * nq + i, h)),
                  pl.BlockSpec((seq, grp * d), lambda b, h, i: (b, ng + h)),
                  pl.BlockSpec((seq, grp * d), lambda b, h, i: (b, 2 * ng + h))],
        out_specs=pl.BlockSpec((tq, grp * d), lambda b, h, i: (b * nq + i, h)),
        out_shape=jax.ShapeDtypeStruct((batch * seq, n_heads * d), BF16),
        scratch_shapes=[pltpu.VMEM((tq, grp * d), F32)],
        compiler_params=_params("parallel", "parallel", "arbitrary"),
        name="stick_breaking",
    )(proj, proj, proj)


def _mem_attn_kernel(q_ref, k_ref, v_ref, o_ref, *, n_heads):
    hd = q_ref.shape[1] // n_heads
    for h in range(n_heads):
        cols = slice(h * hd, (h + 1) * hd)
        s = _dot_nt(q_ref[:, cols], k_ref[:, cols]) * (hd ** -0.5)
        e = jnp.exp(s - jnp.max(s, axis=-1, keepdims=True))
        p = e / jnp.sum(e, axis=-1, keepdims=True)
        o_ref[:, cols] = _dot(p.astype(BF16), v_ref[:, cols]).astype(o_ref.dtype)


def _memory_attention(proj, mq_block, mem_kv, batch, seq, rows=512):
    w = mem_kv.shape[1] // 2
    mtok = mem_kv.shape[0] // batch
    rows = min(rows, seq)
    nt = seq // rows
    return pl.pallas_call(
        functools.partial(_mem_attn_kernel, n_heads=MEM_HEADS),
        grid=(batch, nt),
        in_specs=[pl.BlockSpec((rows, w), lambda b, i: (b * nt + i, mq_block)),
                  pl.BlockSpec((mtok, w), lambda b, i: (b, 0)),
                  pl.BlockSpec((mtok, w), lambda b, i: (b, 1))],
        out_specs=pl.BlockSpec((rows, w), lambda b, i: (b * nt + i, 0)),
        out_shape=jax.ShapeDtypeStruct((batch * seq, w), BF16),
        compiler_params=_params("parallel", "arbitrary"),
        name="memory_attention",
    )(proj, mem_kv, mem_kv)


def _ffn_half_step(h, hg, ssq, w_in, w_out, layer, next_gain):
    act = _swiglu_in(hg, ssq, w_in, layer)
    return _matmul_residual(act, w_out, layer, h, 0.5, next_gain)


def kernel(x, mem, norm_ffn1, ffn1_w_in, ffn1_w_out, norm_mix, a_w_in, a_b_igate, a_b_fgate, a_head_gain, b_w_in, mem_norm, w_mem_kv, w_out, norm_ffn2, ffn2_w_in, ffn2_w_out, norm_final):
    batch, seq, d = x.shape
    mtok = mem.shape[1]
    depth = norm_ffn1.shape[0]
    mem_w = w_mem_kv.shape[1] // 2
    tok_w = w_out.shape[2] - mem_w
    a_heads = tok_w // MLSTM_HEAD_V
    b_heads = tok_w // SB_HEAD_DIM
    a_qkv = 2 * a_heads * MLSTM_HEAD_QK + tok_w

    mem_n = _rmsnorm(mem.reshape(batch * mtok, d), mem_norm, BF16)
    mem_kv = _matmul(mem_n, w_mem_kv[None], 0, 0, 2 * mem_w, BF16)

    h = x.reshape(batch * seq, d)
    hg, ssq = _norm_prep(h, norm_ffn1[0])
    for i in range(depth):
        h, hg, ssq = _ffn_half_step(h, hg, ssq, ffn1_w_in, ffn1_w_out, i, norm_mix[i])
        j = i // 2
        if i % 2 == 0:
            qkv = _matmul(hg, a_w_in, j, 0, a_qkv, BF16, ssq)
            o_gate = _matmul(hg, a_w_in, j, a_qkv, tok_w, F32, ssq)
            tail_w = -(-(2 * a_heads + mem_w) // A_TAIL_TILE) * A_TAIL_TILE
            tail = _matmul(hg, a_w_in, j, a_qkv + tok_w, tail_w, F32, ssq, tn=A_TAIL_TILE)
            mq = tail[:, 2 * a_heads:2 * a_heads + mem_w].astype(BF16)
            gcol, grow = _mlstm_gates(tail, a_b_igate[j], a_b_fgate[j], MLSTM_TILE)
            mixed = _mlstm(qkv, o_gate, gcol, grow, a_head_gain[j], batch, seq, a_heads)
            mem_out = _memory_attention(mq, 0, mem_kv, batch, seq)
        else:
            proj = _matmul(hg, b_w_in, j, 0, 3 * tok_w + mem_w, BF16, ssq)
            mixed = _stick_breaking(proj, batch, seq, b_heads)
            mem_out = _memory_attention(proj, (3 * tok_w) // mem_w, mem_kv, batch, seq)
        h, hg, ssq = _out_proj_residual(mixed, mem_out, w_out, i, h, norm_ffn2[i])
        if i + 1 < depth:
            h, hg, ssq = _ffn_half_step(h, hg, ssq, ffn2_w_in, ffn2_w_out, i, norm_ffn1[i + 1])
        else:
            h = _ffn_half_step(h, hg, ssq, ffn2_w_in, ffn2_w_out, i, None)
    return _rmsnorm(h, norm_final, F32).reshape(batch, seq, d)
```

```python
import functools

import jax
import jax.numpy as jnp
from jax import lax
from jax.experimental import pallas as pl
from jax.experimental.pallas import tpu as pltpu

F32 = jnp.float32
BF16 = jnp.bfloat16

RMS_EPS = 1e-6
GATE_SOFT_CAP = 15.0
MEM_HEADS = 4
MLSTM_HEAD_V = 512
MLSTM_HEAD_QK = 256
SB_HEAD_DIM = 128

V7X_VMEM_LIMIT_BYTES = 56 * 1024 * 1024
LANES = 128
MM_ROWS = 2048
MLSTM_TILE = 256
SB_Q_TILE = 256
SB_K_TILE = 256
SB_HEAD_GROUP = 8
SB_EXP_UNDERFLOW = -104.5
A_TAIL_TILE = 384
GATE_COLS = 128
GATE_ROWS = 16


def _params(*sem):
    return pltpu.CompilerParams(dimension_semantics=sem, vmem_limit_bytes=V7X_VMEM_LIMIT_BYTES)


def _dot(a, b):
    return jnp.dot(a, b, preferred_element_type=F32)


def _dot_nt(a, b):
    return lax.dot_general(a, b, (((1,), (1,)), ((), ())), preferred_element_type=F32)


def _dot_tn(a, b):
    return lax.dot_general(a, b, (((0,), (0,)), ((), ())), preferred_element_type=F32)


def _sigmoid(x):
    return 1.0 / (1.0 + jnp.exp(-x))


def _log_sigmoid(x):
    return jnp.minimum(x, 0.0) - jnp.log1p(jnp.exp(-jnp.abs(x)))


def _split3(x):
    hi = x.astype(BF16)
    r1 = x - hi.astype(F32)
    mid = r1.astype(BF16)
    lo = (r1 - mid.astype(F32)).astype(BF16)
    return hi, mid, lo


def _rmsnorm_kernel(x_ref, g_ref, o_ref):
    x = x_ref[...]
    ms = jnp.mean(x * x, axis=-1, keepdims=True)
    o_ref[...] = ((x * lax.rsqrt(ms + RMS_EPS)) * g_ref[...]).astype(o_ref.dtype)


def _rmsnorm(x, g, out_dtype, rows=256):
    m, d = x.shape
    rows = min(rows, m)
    return pl.pallas_call(
        _rmsnorm_kernel,
        grid=(m // rows,),
        in_specs=[pl.BlockSpec((rows, d), lambda i: (i, 0)),
                  pl.BlockSpec((1, d), lambda i: (0, 0))],
        out_specs=pl.BlockSpec((rows, d), lambda i: (i, 0)),
        out_shape=jax.ShapeDtypeStruct((m, d), out_dtype),
        compiler_params=_params("parallel"),
        name="rmsnorm",
    )(x, g.reshape(1, d))


def _act_spec(tm, k, resident=True):
    if resident:
        return pl.BlockSpec((tm, k), lambda i, j: (i, 0), pipeline_mode=pl.Buffered(1))
    return pl.BlockSpec((tm, k), lambda i, j: (i, 0))


def _w_spec(k, tn, layer, col_block0=0, row_block=0):
    return pl.BlockSpec((None, k, tn), lambda i, j: (layer, row_block, col_block0 + j))


def _scale_rows(x, rs_ref):
    return x * jnp.tile(rs_ref[...], (1, x.shape[1] // LANES))


def _fold_lanes(x):
    out = x[:, :LANES]
    for c in range(LANES, x.shape[1], LANES):
        out = out + x[:, c:c + LANES]
    return out


def _finish_row_scale(folded, width):
    total = jnp.sum(folded, axis=1, keepdims=True)
    return jnp.broadcast_to(lax.rsqrt(total / width + RMS_EPS), folded.shape)


def _emit_norm_inputs(h, g_ref, hg_ref, rs_ref, width):
    hg_ref[...] = (h * g_ref[...]).astype(hg_ref.dtype)
    part = _fold_lanes(h * h)
    j = pl.program_id(1)

    @pl.when(j == 0)
    def _():
        rs_ref[...] = part

    @pl.when(j > 0)
    def _():
        rs_ref[...] += part

    @pl.when(j == width // h.shape[1] - 1)
    def _():
        rs_ref[...] = _finish_row_scale(rs_ref[...], width)


def _norm_out_specs(tm, tn):
    return [pl.BlockSpec((tm, tn), lambda i, j: (i, j)), pl.BlockSpec((tm, LANES), lambda i, j: (i, 0))]


def _norm_out_shapes(m, n):
    return [jax.ShapeDtypeStruct((m, n), BF16), jax.ShapeDtypeStruct((m, LANES), F32)]


def _ssq_spec(tm):
    return pl.BlockSpec((tm, LANES), lambda i, j: (i, 0))


def _norm_prep_kernel(x_ref, g_ref, hg_ref, rs_ref):
    x = x_ref[...]
    hg_ref[...] = (x * g_ref[...]).astype(hg_ref.dtype)
    rs_ref[...] = _finish_row_scale(_fold_lanes(x * x), x.shape[1])


def _norm_prep(x, g, rows=256):
    m, d = x.shape
    return pl.pallas_call(
        _norm_prep_kernel,
        grid=(m // rows,),
        in_specs=[pl.BlockSpec((rows, d), lambda i: (i, 0)),
                  pl.BlockSpec((1, d), lambda i: (0, 0))],
        out_specs=[pl.BlockSpec((rows, d), lambda i: (i, 0)),
                   pl.BlockSpec((rows, LANES), lambda i: (i, 0))],
        out_shape=_norm_out_shapes(m, d),
        compiler_params=_params("parallel"),
        name="norm_prep",
    )(x, g.reshape(1, d))


def _mm_kernel(a_ref, w_ref, o_ref):
    o_ref[...] = _dot(a_ref[...], w_ref[...].astype(BF16)).astype(o_ref.dtype)


def _mm_normed_kernel(a_ref, rs_ref, w_ref, o_ref, *, valid_cols=None):
    w = w_ref[...]
    if valid_cols is not None:
        col = pl.program_id(1) * w.shape[1] + lax.broadcasted_iota(jnp.int32, w.shape, 1)
        w = jnp.where(col < valid_cols, w, 0.0)
    o_ref[...] = _scale_rows(_dot(a_ref[...], w.astype(BF16)), rs_ref).astype(o_ref.dtype)


def _matmul(a, w, layer, col0, ncols, out_dtype, ssq=None, tm=MM_ROWS, tn=512, resident=True):
    m, k = a.shape
    tm = min(tm, m)
    assert col0 % tn == 0 and ncols % tn == 0
    normed = ssq is not None
    edge = w.shape[2] - col0
    body = _mm_kernel
    if normed:
        body = functools.partial(_mm_normed_kernel, valid_cols=edge if edge < ncols else None)
    else:
        assert edge >= ncols
    return pl.pallas_call(
        body,
        grid=(m // tm, ncols // tn),
        in_specs=[_act_spec(tm, k, resident)] + ([_ssq_spec(tm)] if normed else [])
                 + [_w_spec(k, tn, layer, col0 // tn)],
        out_specs=pl.BlockSpec((tm, tn), lambda i, j: (i, j)),
        out_shape=jax.ShapeDtypeStruct((m, ncols), out_dtype),
        compiler_params=_params("parallel", "arbitrary"),
        name="matmul",
    )(*((a, ssq, w) if normed else (a, w)))


def _swiglu_in_kernel(a_ref, rs_ref, wg_ref, wu_ref, o_ref):
    a = a_ref[...]
    half = wg_ref.shape[1] // 2
    halves = [slice(c * half, (c + 1) * half) for c in range(2)]
    prods = [(_dot(a, wg_ref[:, cols].astype(BF16)), _dot(a, wu_ref[:, cols].astype(BF16)))
             for cols in halves]
    for cols, (g, u) in zip(halves, prods):
        g = _scale_rows(g, rs_ref)
        u = _scale_rows(u, rs_ref)
        o_ref[:, cols] = ((g * _sigmoid(g)) * u).astype(o_ref.dtype)


def _swiglu_in(a, ssq, w_in, layer, tm=1024, tn=512, resident=True):
    m, k = a.shape
    f = w_in.shape[2] // 2
    tm = min(tm, m)
    nblk = f // tn
    return pl.pallas_call(
        _swiglu_in_kernel,
        grid=(m // tm, nblk),
        in_specs=[_act_spec(tm, k, resident), _ssq_spec(tm),
                  _w_spec(k, tn, layer), _w_spec(k, tn, layer, nblk)],
        out_specs=pl.BlockSpec((tm, tn), lambda i, j: (i, j)),
        out_shape=jax.ShapeDtypeStruct((m, f), BF16),
        compiler_params=_params("parallel", "arbitrary"),
        name="swiglu_in",
    )(a, ssq, w_in, w_in)


def _mm_residual_kernel(a_ref, w_ref, r_ref, *rest, scale, norm_width):
    h = r_ref[...] + scale * _dot(a_ref[...], w_ref[...].astype(BF16))
    if norm_width:
        g_ref, o_ref, hg_ref, rs_ref = rest
        _emit_norm_inputs(h, g_ref, hg_ref, rs_ref, norm_width)
    else:
        o_ref, = rest
    o_ref[...] = h


def _matmul_residual(a, w, layer, res, scale, next_gain=None, tm=1024, tn=256, resident=False):
    m, k = a.shape
    n = w.shape[2]
    tm = min(tm, m)
    emit = next_gain is not None
    tile = pl.BlockSpec((tm, tn), lambda i, j: (i, j))
    out = pl.pallas_call(
        functools.partial(_mm_residual_kernel, scale=scale, norm_width=n if emit else 0),
        grid=(m // tm, n // tn),
        in_specs=[_act_spec(tm, k, resident), _w_spec(k, tn, layer), tile]
                 + ([pl.BlockSpec((1, tn), lambda i, j: (0, j))] if emit else []),
        out_specs=[tile] + (_norm_out_specs(tm, tn) if emit else []),
        out_shape=[jax.ShapeDtypeStruct((m, n), F32)] + (_norm_out_shapes(m, n) if emit else []),
        compiler_params=_params("parallel", "arbitrary"),
        name="matmul_residual",
    )(*((a, w, res, next_gain.reshape(1, n)) if emit else (a, w, res)))
    return out if emit else out[0]


def _mm2_residual_kernel(a1_ref, a2_ref, w1_ref, w2_ref, r_ref, g_ref, o_ref, hg_ref, rs_ref, *, norm_width):
    half = w1_ref.shape[1] // 2
    halves = [slice(c * half, (c + 1) * half) for c in range(2)]
    a1, a2 = a1_ref[...], a2_ref[...]
    accs = [_dot(a1, w1_ref[:, cols].astype(BF16)) + _dot(a2, w2_ref[:, cols].astype(BF16))
            for cols in halves]
    h = jnp.concatenate([r_ref[:, cols] + acc for cols, acc in zip(halves, accs)], axis=1)
    o_ref[...] = h
    _emit_norm_inputs(h, g_ref, hg_ref, rs_ref, norm_width)


def _out_proj_residual(a1, a2, w, layer, res, next_gain, tm=1024, tn=512, resident=False):
    m, k1 = a1.shape
    k2 = a2.shape[1]
    n = w.shape[2]
    assert k1 % k2 == 0 and w.shape[1] == k1 + k2
    tm = min(tm, m)
    tile = pl.BlockSpec((tm, tn), lambda i, j: (i, j))
    return pl.pallas_call(
        functools.partial(_mm2_residual_kernel, norm_width=n),
        grid=(m // tm, n // tn),
        in_specs=[_act_spec(tm, k1, resident), _act_spec(tm, k2, resident),
                  _w_spec(k1, tn, layer), _w_spec(k2, tn, layer, row_block=k1 // k2),
                  tile, pl.BlockSpec((1, tn), lambda i, j: (0, j))],
        out_specs=[tile] + _norm_out_specs(tm, tn),
        out_shape=[jax.ShapeDtypeStruct((m, n), F32)] + _norm_out_shapes(m, n),
        compiler_params=_params("parallel", "arbitrary"),
        name="out_proj_residual",
    )(a1, a2, w, w, res, next_gain.reshape(1, n))


def _gates_kernel(p_ref, brow_ref, bcol_ref, col_ref, row_ref, *, n_heads, tile):
    p = p_ref[...]
    rows = p.shape[0]
    g_col = p + brow_ref[...]
    g_row = p.T[:GATE_ROWS, :] + bcol_ref[...]

    def transform(t, is_forget):
        capped = GATE_SOFT_CAP * jnp.tanh(t / GATE_SOFT_CAP)
        return jnp.where(is_forget, _log_sigmoid(capped), capped)

    lane = lax.broadcasted_iota(jnp.int32, g_col.shape, 1)
    sub = lax.broadcasted_iota(jnp.int32, g_row.shape, 0)
    t_col = transform(g_col, lane >= n_heads)
    t_row = transform(g_row, sub >= n_heads)

    r = lax.broadcasted_iota(jnp.int32, (tile, tile), 0)
    c = lax.broadcasted_iota(jnp.int32, (tile, tile), 1)
    lower = (c <= r).astype(BF16)
    upper = (r <= c).astype(BF16)
    forget_lane = lax.broadcasted_iota(jnp.int32, (tile, g_col.shape[1]), 1) >= n_heads
    forget_sub = lax.broadcasted_iota(jnp.int32, (g_row.shape[0], tile), 0) >= n_heads

    for t0 in range(0, rows, tile):
        blk = t_col[t0:t0 + tile, :]
        hi, mid, lo = _split3(blk)
        cs = _dot(lower, hi) + _dot(lower, mid) + _dot(lower, lo)
        col_ref[t0:t0 + tile, :] = jnp.where(forget_lane, cs, blk)

        blk = t_row[:, t0:t0 + tile]
        hi, mid, lo = _split3(blk)
        cs = _dot(hi, upper) + _dot(mid, upper) + _dot(lo, upper)
        row_ref[:, t0:t0 + tile] = jnp.where(forget_sub, cs, blk)


def _mlstm_gates(tail, b_ig, b_fg, tile, rows=512):
    m = tail.shape[0]
    n_heads = b_ig.shape[0]
    rows = min(rows, m)
    bias = jnp.concatenate([b_ig, b_fg]).astype(F32)
    brow = jnp.zeros((1, GATE_COLS), F32).at[0, :2 * n_heads].set(bias)
    bcol = jnp.zeros((GATE_ROWS, 1), F32).at[:2 * n_heads, 0].set(bias)
    return pl.pallas_call(
        functools.partial(_gates_kernel, n_heads=n_heads, tile=tile),
        grid=(m // rows,),
        in_specs=[pl.BlockSpec((rows, GATE_COLS), lambda i: (i, 0)),
                  pl.BlockSpec((1, GATE_COLS), lambda i: (0, 0)),
                  pl.BlockSpec((GATE_ROWS, 1), lambda i: (0, 0))],
        out_specs=[pl.BlockSpec((rows, GATE_COLS), lambda i: (i, 0)),
                   pl.BlockSpec((GATE_ROWS, rows), lambda i: (0, i))],
        out_shape=[jax.ShapeDtypeStruct((m, GATE_COLS), F32),
                   jax.ShapeDtypeStruct((GATE_ROWS, m), F32)],
        compiler_params=_params("parallel"),
        name="mlstm_gates",
    )(tail, brow, bcol)


def _mlstm_kernel(q_ref, k_ref, v_ref, o_ref, gcol_ref, grow_ref, gain_ref, out_ref,
                  c_ref, n_ref, m_ref, *, n_heads, dk, dv):
    @pl.when(pl.program_id(1) == 0)
    def _():
        c_ref[...] = jnp.zeros_like(c_ref)
        n_ref[...] = jnp.zeros_like(n_ref)
        m_ref[...] = jnp.zeros_like(m_ref)

    t = q_ref.shape[0]
    r = lax.broadcasted_iota(jnp.int32, (t, t), 0)
    c = lax.broadcasted_iota(jnp.int32, (t, t), 1)
    causal = c <= r
    gcol = gcol_ref[...]
    grow = grow_ref[...]

    for h in range(n_heads):
        qb = q_ref[:, h * dk:(h + 1) * dk]
        kb = k_ref[:, h * dk:(h + 1) * dk]
        vb = v_ref[:, h * dv:(h + 1) * dv]
        q = qb.astype(F32) * (dk ** -0.5)
        qb = q.astype(BF16)
        k = kb.astype(F32)
        v = vb.astype(F32)
        ig_col = gcol[:, h:h + 1]
        b_col = gcol[:, n_heads + h:n_heads + h + 1]
        ig_row = grow[h:h + 1, :]
        b_row = grow[n_heads + h:n_heads + h + 1, :]
        c_prev = c_ref[h]
        n_prev = n_ref[h]
        m_prev = m_ref[h]

        d_log = jnp.where(causal, (b_col - b_row) + ig_row, -jnp.inf)
        inter_log = b_col + m_prev
        m_t = jnp.maximum(inter_log, jnp.max(d_log, axis=1, keepdims=True))
        d_w = jnp.exp(d_log - m_t)
        inter_w = jnp.exp(inter_log - m_t)
        s = _dot_nt(qb, kb) * d_w
        num = _dot(s.astype(BF16), vb) + inter_w * _dot(qb, c_prev.astype(BF16))
        den = jnp.sum(s, axis=1, keepdims=True) + inter_w * jnp.sum(q * n_prev, axis=1, keepdims=True)
        hh = num / jnp.maximum(jnp.abs(den), jnp.exp(-m_t))

        b_last = b_col[t - 1:t, :]
        w_log = (b_last - b_col) + ig_col
        m_new = jnp.maximum(b_last + m_prev, jnp.max(w_log, axis=0, keepdims=True))
        decay = jnp.exp((b_last + m_prev) - m_new)
        w = jnp.exp(w_log - m_new)
        c_ref[h] = decay * c_prev + _dot_tn(kb, (w * v).astype(BF16))
        n_ref[h] = decay * n_prev + jnp.sum(w * k, axis=0, keepdims=True)
        m_ref[h] = m_new

        ms = jnp.mean(hh * hh, axis=-1, keepdims=True)
        y = (hh * lax.rsqrt(ms + RMS_EPS)) * gain_ref[:, h * dv:(h + 1) * dv]
        o = o_ref[:, h * dv:(h + 1) * dv]
        out_ref[:, h * dv:(h + 1) * dv] = (_sigmoid(o) * y).astype(out_ref.dtype)


def _mlstm(qkv, o_gate, gcol, grow, head_gain, batch, seq, n_heads):
    dk, dv, t = MLSTM_HEAD_QK, MLSTM_HEAD_V, MLSTM_TILE
    wqk, wv = n_heads * dk, n_heads * dv
    assert 2 * wqk == wv
    nt = seq // t
    row = lambda b, i: b * nt + i
    return pl.pallas_call(
        functools.partial(_mlstm_kernel, n_heads=n_heads, dk=dk, dv=dv),
        grid=(batch, nt),
        in_specs=[pl.BlockSpec((t, wqk), lambda b, i: (row(b, i), 0)),
                  pl.BlockSpec((t, wqk), lambda b, i: (row(b, i), 1)),
                  pl.BlockSpec((t, wv), lambda b, i: (row(b, i), 1)),
                  pl.BlockSpec((t, wv), lambda b, i: (row(b, i), 0)),
                  pl.BlockSpec((t, GATE_COLS), lambda b, i: (row(b, i), 0)),
                  pl.BlockSpec((GATE_ROWS, t), lambda b, i: (0, row(b, i))),
                  pl.BlockSpec((1, wv), lambda b, i: (0, 0))],
        out_specs=pl.BlockSpec((t, wv), lambda b, i: (row(b, i), 0)),
        out_shape=jax.ShapeDtypeStruct((batch * seq, wv), BF16),
        scratch_shapes=[pltpu.VMEM((n_heads, dk, dv), F32),
                        pltpu.VMEM((n_heads, 1, dk), F32),
                        pltpu.VMEM((n_heads, 1, 1), F32)],
        compiler_params=_params("parallel", "arbitrary"),
        name="mlstm",
    )(qkv, qkv, qkv, o_gate, gcol, grow, head_gain.reshape(1, wv))


def _sb_kernel(q_ref, k_ref, v_ref, o_ref, acc_ref, *, scale, tk, d):
    tq = q_ref.shape[0]
    heads = q_ref.shape[1] // d
    qi = pl.program_id(2)
    r = lax.broadcasted_iota(jnp.int32, (tk, tk), 0)
    c = lax.broadcasted_iota(jnp.int32, (tk, tk), 1)
    strictly_later = (r > c).astype(BF16)

    def tile(j, carries, diagonal):
        k0 = pl.multiple_of(j * tk, tk)
        if diagonal:
            t_pos = qi * tq + lax.broadcasted_iota(jnp.int32, (tq, tk), 0)
            s_pos = j * tk + lax.broadcasted_iota(jnp.int32, (tq, tk), 1)
            causal = s_pos < t_pos
        col = [slice(g * d, (g + 1) * d) for g in range(heads)]
        zs = [_dot_nt(q_ref[:, col[g]], k_ref[pl.ds(k0, tk), col[g]]) * scale for g in range(heads)]
        log_betas, log_keeps, splits = [], [], []
        for z in zs:
            log_beta = jnp.minimum(z, 0.0) - jnp.log(1.0 + jnp.exp(-jnp.abs(z)))
            log_keep = log_beta - z
            if diagonal:
                log_keep = jnp.where(causal, log_keep, 0.0)
            hi = log_keep.astype(BF16)
            lo = (log_keep - hi.astype(F32)).astype(BF16)
            log_betas.append(log_beta)
            log_keeps.append(log_keep)
            splits.append((hi, lo))
        laters = [_dot(hi, strictly_later) + _dot(lo, strictly_later) for hi, lo in splits]
        probs = []
        for g in range(heads):
            a = jnp.exp(log_betas[g] + (laters[g] + carries[g]))
            if diagonal:
                a = jnp.where(causal, a, 0.0)
            probs.append(a.astype(BF16))
        pv = jnp.concatenate([_dot(probs[g], v_ref[pl.ds(k0, tk), col[g]]) for g in range(heads)], axis=1)
        acc_ref[...] = pv if diagonal else acc_ref[...] + pv
        return tuple(carries[g] + jnp.sum(log_keeps[g], axis=1, keepdims=True) for g in range(heads))

    def alive(carries):
        top = functools.reduce(jnp.maximum, carries)
        return (jnp.max(top) > SB_EXP_UNDERFLOW).astype(jnp.int32)

    def body(state):
        i, _, carries = state
        carries = tile(j_diag - 1 - i, carries, False)
        return i + 1, alive(carries), carries

    j_diag = (qi * tq + tq - 1) // tk
    carries = tile(j_diag, tuple(jnp.zeros((tq, 1), F32) for _ in range(heads)), True)
    lax.while_loop(lambda s: (s[0] < j_diag) & (s[1] > 0), body, (0, alive(carries), carries))
    o_ref[...] = acc_ref[...].astype(o_ref.dtype)


def _stick_breaking(proj, batch, seq, n_heads, tq=SB_Q_TILE, tk=SB_K_TILE, grp=SB_HEAD_GROUP):
    d, tk = SB_HEAD_DIM, min(tk, seq)
    assert tk % tq == 0, "only one key tile may cross the causal diagonal"
    assert n_heads % grp == 0 and seq % tk == 0
    nq = seq // tq
    ng = n_heads // grp
    return pl.pallas_call(
        functools.partial(_sb_kernel, scale=d ** -0.5, tk=tk, d=d),
        grid=(batch, ng, nq),
        in_specs=[pl.BlockSpec((tq, grp * d), lambda b, h, i: (b * nq + i, h)),
                  pl.BlockSpec((seq, grp * d), lambda b, h, i: (b, ng + h)),
                  pl.BlockSpec((seq, grp * d), lambda b, h, i: (b, 2 * ng + h))],
        out_specs=pl.BlockSpec((tq, grp * d), lambda b, h, i: (b * nq + i, h)),
        out_shape=jax.ShapeDtypeStruct((batch * seq, n_heads * d), BF16),
        scratch_shapes=[pltpu.VMEM((tq, grp * d), F32)],
        compiler_params=_params("parallel", "parallel", "arbitrary"),
        name="stick_breaking",
    )(proj, proj, proj)


def _mem_attn_kernel(q_ref, k_ref, v_ref, o_ref, *, n_heads):
    hd = q_ref.shape[1] // n_heads
    for h in range(n_heads):
        cols = slice(h * hd, (h + 1) * hd)
        s = _dot_nt(q_ref[:, cols], k_ref[:, cols]) * (hd ** -0.5)
        e = jnp.exp(s - jnp.max(s, axis=-1, keepdims=True))
        p = e / jnp.sum(e, axis=-1, keepdims=True)
        o_ref[:, cols] = _dot(p.astype(BF16), v_ref[:, cols]).astype(o_ref.dtype)


def _memory_attention(proj, mq_block, mem_kv, batch, seq, rows=512):
    w = mem_kv.shape[1] // 2
    mtok = mem_kv.shape[0] // batch
    rows = min(rows, seq)
    nt = seq // rows
    return pl.pallas_call(
        functools.partial(_mem_attn_kernel, n_heads=MEM_HEADS),
        grid=(batch, nt),
        in_specs=[pl.BlockSpec((rows, w), lambda b, i: (b * nt + i, mq_block)),
                  pl.BlockSpec((mtok, w), lambda b, i: (b, 0)),
                  pl.BlockSpec((mtok, w), lambda b, i: (b, 1))],
        out_specs=pl.BlockSpec((rows, w), lambda b, i: (b * nt + i, 0)),
        out_shape=jax.ShapeDtypeStruct((batch * seq, w), BF16),
        compiler_params=_params("parallel", "arbitrary"),
        name="memory_attention",
    )(proj, mem_kv, mem_kv)


def _ffn_half_step(h, hg, ssq, w_in, w_out, layer, next_gain):
    act = _swiglu_in(hg, ssq, w_in, layer)
    return _matmul_residual(act, w_out, layer, h, 0.5, next_gain)


def kernel(x, mem, norm_ffn1, ffn1_w_in, ffn1_w_out, norm_mix, a_w_in, a_b_igate, a_b_fgate, a_head_gain, b_w_in, mem_norm, w_mem_kv, w_out, norm_ffn2, ffn2_w_in, ffn2_w_out, norm_final):
    batch, seq, d = x.shape
    mtok = mem.shape[1]
    depth = norm_ffn1.shape[0]
    mem_w = w_mem_kv.shape[1] // 2
    tok_w = w_out.shape[2] - mem_w
    a_heads = tok_w // MLSTM_HEAD_V
    b_heads = tok_w // SB_HEAD_DIM
    a_qkv = 2 * a_heads * MLSTM_HEAD_QK + tok_w

    mem_n = _rmsnorm(mem.reshape(batch * mtok, d), mem_norm, BF16)
    mem_kv = _matmul(mem_n, w_mem_kv[None], 0, 0, 2 * mem_w, BF16)

    h = x.reshape(batch * seq, d)
    hg, ssq = _norm_prep(h, norm_ffn1[0])
    for i in range(depth):
        h, hg, ssq = _ffn_half_step(h, hg, ssq, ffn1_w_in, ffn1_w_out, i, norm_mix[i])
        j = i // 2
        if i % 2 == 0:
            qkv = _matmul(hg, a_w_in, j, 0, a_qkv, BF16, ssq)
            o_gate = _matmul(hg, a_w_in, j, a_qkv, tok_w, F32, ssq)
            tail_w = -(-(2 * a_heads + mem_w) // A_TAIL_TILE) * A_TAIL_TILE
            tail = _matmul(hg, a_w_in, j, a_qkv + tok_w, tail_w, F32, ssq, tn=A_TAIL_TILE)
            mq = tail[:, 2 * a_heads:2 * a_heads + mem_w].astype(BF16)
            gcol, grow = _mlstm_gates(tail, a_b_igate[j], a_b_fgate[j], MLSTM_TILE)
            mixed = _mlstm(qkv, o_gate, gcol, grow, a_head_gain[j], batch, seq, a_heads)
            mem_out = _memory_attention(mq, 0, mem_kv, batch, seq)
        else:
            proj = _matmul(hg, b_w_in, j, 0, 3 * tok_w + mem_w, BF16, ssq)
            mixed = _stick_breaking(proj, batch, seq, b_heads)
            mem_out = _memory_attention(proj, (3 * tok_w) // mem_w, mem_kv, batch, seq)
        h, hg, ssq = _out_proj_residual(mixed, mem_out, w_out, i, h, norm_ffn2[i])
        if i + 1 < depth:
            h, hg, ssq = _ffn_half_step(h, hg, ssq, ffn2_w_in, ffn2_w_out, i, norm_ffn1[i + 1])
        else:
            h = _ffn_half_step(h, hg, ssq, ffn2_w_in, ffn2_w_out, i, None)
    return _rmsnorm(h, norm_final, F32).reshape(batch, seq, d)
```
